```python
import jax, jax.numpy as jnp
from jax import lax
import numpy as np

D_MODEL = 1024
BATCH = 16
SEQ = 256
DEPTH = 2
DEC_BATCH = 8
DEC_SEQ = 4096
PAST_LEN = 512

GRID_W = 64
ML_HEADS = 4
ML_HEAD_DIM = 128
ML_WIDTH = ML_HEADS * ML_HEAD_DIM
ML_CHUNK = 64
CV_WIDTH = 512
CV_TAPS = 3
NA_HEADS = 8
NA_HEAD_DIM = 64
NA_WIDTH = NA_HEADS * NA_HEAD_DIM
NA_WIN_ROWS = 8
NA_WIN_COLS = 16
NA_QBLOCK = 16
NA_KBAND = 32
N_BRANCH = 3
N_EXPERTS = 16
EC_CAPACITY_FACTOR = 2
MOE_FF = 1024
ROPE_BASE = 10000.0
NORM_EPS = 1e-6
PROJ_SIZES = (ML_WIDTH, ML_WIDTH, ML_WIDTH, ML_WIDTH, 2 * ML_HEADS, 2 * ML_HEADS,
              CV_WIDTH, CV_WIDTH, CV_WIDTH, NA_WIDTH, NA_WIDTH, NA_WIDTH, N_BRANCH * D_MODEL)
PROJ_WIDTH = sum(PROJ_SIZES)

kernel_name = 'hybrid_mlstm_conv_natten_ec_diffusion_step'


def rmsnorm(x, g):
    xf = x.astype(jnp.float32)
    y = xf * lax.rsqrt(jnp.mean(xf * xf, axis=-1, keepdims=True) + NORM_EPS)
    return (y * g.astype(jnp.float32)).astype(x.dtype)


def head_rms(x, g):
    return rmsnorm(x, g)


def modulation(cvec, w, b):
    m = jax.nn.silu(cvec) @ w + b
    return jnp.split(m, 6, axis=-1)


def rope_1d(x, pos):
    half = x.shape[-1] // 2
    freqs = ROPE_BASE ** (-jnp.arange(half, dtype=jnp.float32) / half)
    ang = pos[:, None] * freqs[None, :]
    cos = jnp.cos(ang)[None, :, None, :]
    sin = jnp.sin(ang)[None, :, None, :]
    x1, x2 = x[..., :half], x[..., half:]
    return jnp.concatenate([x1 * cos - x2 * sin, x1 * sin + x2 * cos], axis=-1).astype(x.dtype)


def rope_2d(x):
    t = jnp.arange(x.shape[1])
    row = (t // GRID_W).astype(jnp.float32)
    col = (t % GRID_W).astype(jnp.float32)
    half = x.shape[-1] // 2
    return jnp.concatenate([rope_1d(x[..., :half], row), rope_1d(x[..., half:], col)], axis=-1)


def mlstm_chunk_scan(q, k, v, ig, lf, c0, n0, m0):
    bsz, n_tok, n_h, d = q.shape
    n_chunk = n_tok // ML_CHUNK

    def chunks4(a):
        return a.reshape(bsz, n_chunk, ML_CHUNK, n_h, a.shape[-1]).transpose(1, 0, 3, 2, 4)

    def chunks3(a):
        return a.reshape(bsz, n_chunk, ML_CHUNK, n_h).transpose(1, 0, 3, 2)

    lower = jnp.tril(jnp.ones((ML_CHUNK, ML_CHUNK), dtype=bool))

    def step(carry, xs):
        c_prev, n_prev, m_prev = carry
        qc, kc, vc, ic, fc = xs
        b = jnp.cumsum(fc, axis=-1)
        dlog = jnp.where(lower, b[..., :, None] - b[..., None, :] + ic[..., None, :], -jnp.inf)
        m_t = jnp.maximum(b + m_prev[..., None], jnp.max(dlog, axis=-1))
        inter = jnp.exp(b + m_prev[..., None] - m_t)
        s = jnp.einsum('bhtd,bhsd->bhts', qc, kc) * jnp.exp(dlog - m_t[..., None])
        num = inter[..., None] * jnp.einsum('bhtd,bhdv->bhtv', qc, c_prev) + jnp.einsum('bhts,bhsv->bhtv', s, vc)
        den = inter * jnp.einsum('bhtd,bhd->bht', qc, n_prev) + jnp.sum(s, axis=-1)
        h = num / jnp.maximum(jnp.abs(den), jnp.exp(-m_t))[..., None]
        m_new = m_t[..., -1]
        w_end = jnp.exp(b[..., -1:] - b + ic - m_new[..., None])
        decay = jnp.exp(b[..., -1] + m_prev - m_new)
        c_new = decay[..., None, None] * c_prev + jnp.einsum('bhs,bhsd,bhsv->bhdv', w_end, kc, vc)
        n_new = decay[..., None] * n_prev + jnp.einsum('bhs,bhsd->bhd', w_end, kc)
        return (c_new, n_new, m_new), h

    (c_f, n_f, m_f), hs = lax.scan(step, (c0, n0, m0),
                                   (chunks4(q), chunks4(k), chunks4(v), chunks3(ig), chunks3(lf)))
    h = hs.transpose(1, 0, 3, 2, 4).reshape(bsz, n_tok, n_h, d)
    return h, c_f, n_f, m_f


def mlstm_bidir(q, k, v, ig, lf, c0, n0, m0):
    rev = lambda a: jnp.flip(a, axis=1)
    h_f, cf, nf, mf = mlstm_chunk_scan(q, k, v, ig[:, :, 0], lf[:, :, 0], c0[:, 0], n0[:, 0], m0[:, 0])
    h_b, cb, nb, mb = mlstm_chunk_scan(rev(q), rev(k), rev(v), rev(ig[:, :, 1]), rev(lf[:, :, 1]),
                                       c0[:, 1], n0[:, 1], m0[:, 1])
    return (h_f + rev(h_b), jnp.stack([cf, cb], axis=1), jnp.stack([nf, nb], axis=1),
            jnp.stack([mf, mb], axis=1))


def short_conv(x, w):
    pad = CV_TAPS // 2
    xp = jnp.pad(x, ((0, 0), (pad, pad), (0, 0)))
    n = x.shape[1]
    return sum(xp[:, j:j + n] * w[j] for j in range(CV_TAPS))


def na_context(q, k, v):
    s = jnp.einsum('bqhd,bkhd->bhqk', q, k, preferred_element_type=jnp.float32) * NA_HEAD_DIM ** -0.5
    p = jax.nn.softmax(s, axis=-1)
    return jnp.einsum('bhqk,bkhd->bqhd', p.astype(v.dtype), v)


def na_latent(q, k, v, ck, cv, rpb):
    bsz, n_tok, n_h, dh = q.shape
    rows = n_tok // GRID_W
    kh = min(NA_WIN_ROWS, rows)
    ncb = GRID_W // NA_QBLOCK
    scale = NA_HEAD_DIM ** -0.5
    qb = q.reshape(bsz, rows, ncb, NA_QBLOCK, n_h, dh)
    kg = k.reshape(bsz, rows, GRID_W, n_h, dh)
    vg = v.reshape(bsz, rows, GRID_W, n_h, dh)
    r = jnp.arange(rows)
    row_idx = jnp.clip(r - NA_WIN_ROWS // 2, 0, rows - kh)[:, None] + jnp.arange(kh)[None, :]
    col_start = jnp.clip(jnp.arange(GRID_W) - NA_WIN_COLS // 2, 0, GRID_W - NA_WIN_COLS)
    band_start = jnp.clip(col_start[::NA_QBLOCK], 0, GRID_W - NA_KBAND)
    band_idx = band_start[:, None] + jnp.arange(NA_KBAND)[None, :]
    ri = row_idx[:, :, None, None]
    bi = band_idx[None, None]
    k_blk = kg[:, ri, bi]
    v_blk = vg[:, ri, bi]
    sw = jnp.einsum('brnqhd,brinjhd->bhrnqij', qb, k_blk, preferred_element_type=jnp.float32) * scale
    qcol = jnp.arange(ncb)[:, None] * NA_QBLOCK + jnp.arange(NA_QBLOCK)[None, :]
    qs = col_start[qcol]
    kcol = band_idx[:, None, :]
    valid = (kcol >= qs[..., None]) & (kcol < qs[..., None] + NA_WIN_COLS)
    dr = row_idx - r[:, None] + NA_WIN_ROWS - 1
    dc = jnp.clip(kcol - qcol[..., None] + NA_WIN_COLS - 1, 0, 2 * NA_WIN_COLS - 2)
    bias = rpb[:, dr[:, None, None, :, None], dc[None, :, :, None, :]]
    sw = jnp.where(valid[:, :, None, :], sw + bias[None].astype(jnp.float32), -jnp.inf)
    sc = jnp.einsum('brnqhd,bmhd->bhrnqm', qb, ck, preferred_element_type=jnp.float32) * scale
    n_win = kh * NA_KBAND
    joint = jnp.concatenate([sw.reshape(sw.shape[:5] + (n_win,)), sc], axis=-1)
    p = jax.nn.softmax(joint, axis=-1).astype(v.dtype)
    pw = p[..., :n_win].reshape(sw.shape)
    pc = p[..., n_win:]
    out = (jnp.einsum('bhrnqij,brinjhd->brnqhd', pw, v_blk)
           + jnp.einsum('bhrnqm,bmhd->brnqhd', pc, cv))
    return out.reshape(bsz, n_tok, n_h, dh)


def token_mixer(h, ml_state, na_ctx, w_in, b_in, ml_fbias, ml_out_g, cv_w, na_q_g, na_k_g,
                w_pa, w_pb, w_pc, w_o):
    latent = na_ctx is not None
    bsz, n_tok, _ = h.shape
    f32 = jnp.float32
    p = h @ w_in + b_in
    (mq, mk, mv, mo, mi, mf, cb, cc, ch, nq, nk, nv, gt) = jnp.split(
        p, np.cumsum(PROJ_SIZES)[:-1].tolist(), axis=-1)
    ml_heads = lambda a: a.reshape(bsz, n_tok, ML_HEADS, ML_HEAD_DIM).astype(f32)
    mq, mk, mv = ml_heads(mq), ml_heads(mk), ml_heads(mv)
    if latent:
        mq, mk = rope_2d(mq), rope_2d(mk)
    mk = mk * ML_HEAD_DIM ** -0.5
    ig = mi.reshape(bsz, n_tok, 2, ML_HEADS).astype(f32)
    lf = jax.nn.log_sigmoid(mf.reshape(bsz, n_tok, 2, ML_HEADS).astype(f32) + ml_fbias.astype(f32))
    h_ml, c_s, n_s, m_s = mlstm_bidir(mq, mk, mv, ig, lf, ml_state[0], ml_state[1], ml_state[2])
    y_ml = head_rms(h_ml, ml_out_g.reshape(ML_HEADS, ML_HEAD_DIM)).reshape(
        bsz, n_tok, ML_WIDTH).astype(h.dtype) * jax.nn.sigmoid(mo)
    y_cv = cb * short_conv(cc * ch, cv_w)
    na_heads = lambda a: a.reshape(bsz, n_tok, NA_HEADS, NA_HEAD_DIM)
    nq = head_rms(na_heads(nq), na_q_g)
    nk = head_rms(na_heads(nk), na_k_g)
    nv = na_heads(nv)
    if latent:
        y_na = na_latent(nq, nk, nv, na_ctx[0], na_ctx[1], na_ctx[2])
    else:
        y_na = na_context(nq, nk, nv)
    y_na = y_na.reshape(bsz, n_tok, NA_WIDTH)
    g = jax.nn.sigmoid(gt.reshape(bsz, n_tok, N_BRANCH, D_MODEL))
    merged = g[:, :, 0] * (y_ml @ w_pa) + g[:, :, 1] * (y_cv @ w_pb) + g[:, :, 2] * (y_na @ w_pc)
    return merged @ w_o, (c_s, n_s, m_s), nk, nv


def expert_choice_ffn(h, w_router, w_gate, w_up, w_down):
    bsz, n_tok, d = h.shape
    cap = EC_CAPACITY_FACTOR * n_tok // N_EXPERTS
    aff = jax.nn.softmax((h @ w_router).astype(jnp.float32), axis=-1)
    gate, idx = lax.top_k(jnp.swapaxes(aff, 1, 2), cap)
    xg = jax.vmap(lambda hb, ib: hb[ib])(h, idx)
    a = jnp.einsum('becd,edf->becf', xg, w_gate)
    u = jnp.einsum('becd,edf->becf', xg, w_up)
    y = jnp.einsum('becf,efd->becd', jax.nn.silu(a) * u, w_down) * gate[..., None].astype(h.dtype)
    return jax.vmap(lambda yb, ib: jnp.zeros((n_tok, d), h.dtype).at[ib.reshape(-1)].add(
        yb.reshape(-1, d)))(y, idx)


def setup_inputs(seed: int = 0) -> dict:
    key = jax.random.key(seed)
    ks = jax.random.split(key, 32)

    def nrm(i, shape, s=1.0):
        return jax.random.normal(ks[i], shape, jnp.float32) * s

    return {
        'x_prompt': nrm(0, (BATCH, SEQ, D_MODEL)),
        'x_sample': nrm(1, (DEC_BATCH, DEC_SEQ, D_MODEL)),
        'c': nrm(2, (DEC_BATCH, D_MODEL)),
        'cache_ctx_k': nrm(3, (DEC_BATCH, DEPTH, PAST_LEN, NA_HEADS, NA_HEAD_DIM)),
        'cache_ctx_v': nrm(4, (DEC_BATCH, DEPTH, PAST_LEN, NA_HEADS, NA_HEAD_DIM)),
        'state_mlstm_c': nrm(5, (DEC_BATCH, DEPTH, 2, ML_HEADS, ML_HEAD_DIM, ML_HEAD_DIM), 0.3),
        'state_mlstm_n': nrm(6, (DEC_BATCH, DEPTH, 2, ML_HEADS, ML_HEAD_DIM), 0.3),
        'state_mlstm_m': jax.random.uniform(ks[7], (DEC_BATCH, DEPTH, 2, ML_HEADS), jnp.float32, 0.0, 3.0),
        'c_ctx': nrm(8, (D_MODEL,)),
        'ada_w': nrm(9, (DEPTH, D_MODEL, 6 * D_MODEL), 0.3 * D_MODEL ** -0.5),
        'ada_b': nrm(10, (DEPTH, 6 * D_MODEL), 0.02),
        'norm1_g': 1.0 + nrm(11, (DEPTH, D_MODEL), 0.02),
        'norm2_g': 1.0 + nrm(12, (DEPTH, D_MODEL), 0.02),
        'w_in': nrm(13, (DEPTH, D_MODEL, PROJ_WIDTH), D_MODEL ** -0.5),
        'b_in': nrm(14, (DEPTH, PROJ_WIDTH), 0.02),
        'ml_fbias': 3.0 + nrm(15, (DEPTH, 2, ML_HEADS), 0.5),
        'ml_out_g': 1.0 + nrm(16, (DEPTH, ML_WIDTH), 0.02),
        'cv_w': nrm(17, (DEPTH, CV_TAPS, CV_WIDTH), CV_TAPS ** -0.5),
        'na_q_g': 1.0 + nrm(18, (DEPTH, NA_HEAD_DIM), 0.02),
        'na_k_g': 1.0 + nrm(19, (DEPTH, NA_HEAD_DIM), 0.02),
        'na_rpb': nrm(20, (DEPTH, NA_HEADS, 2 * NA_WIN_ROWS - 1, 2 * NA_WIN_COLS - 1), 0.02),
        'w_pa': nrm(21, (DEPTH, ML_WIDTH, D_MODEL), ML_WIDTH ** -0.5),
        'w_pb': nrm(22, (DEPTH, CV_WIDTH, D_MODEL), CV_WIDTH ** -0.5),
        'w_pc': nrm(23, (DEPTH, NA_WIDTH, D_MODEL), NA_WIDTH ** -0.5),
        'w_o': nrm(24, (DEPTH, D_MODEL, D_MODEL), D_MODEL ** -0.5),
        'w_router': nrm(25, (DEPTH, D_MODEL, N_EXPERTS), D_MODEL ** -0.5),
        'w_gate': nrm(26, (DEPTH, N_EXPERTS, D_MODEL, MOE_FF), D_MODEL ** -0.5),
        'w_up': nrm(27, (DEPTH, N_EXPERTS, D_MODEL, MOE_FF), D_MODEL ** -0.5),
        'w_down': nrm(28, (DEPTH, N_EXPERTS, MOE_FF, D_MODEL), MOE_FF ** -0.5),
    }


def reference(x_prompt, x_sample, c, cache_ctx_k, cache_ctx_v, state_mlstm_c, state_mlstm_n,
              state_mlstm_m, c_ctx, ada_w, ada_b, norm1_g, norm2_g, w_in, b_in, ml_fbias, ml_out_g,
              cv_w, na_q_g, na_k_g, na_rpb, w_pa, w_pb, w_pc, w_o, w_router, w_gate, w_up, w_down):
    f32 = jnp.float32
    xp, xs = x_prompt, x_sample
    bp = xp.shape[0]
    ks_l, vs_l, cs_l, ns_l, ms_l = [], [], [], [], []
    for l in range(DEPTH):
        lw = (w_in[l], b_in[l], ml_fbias[l], ml_out_g[l], cv_w[l], na_q_g[l], na_k_g[l],
              w_pa[l], w_pb[l], w_pc[l], w_o[l])
        sh1, sc1, gt1, sh2, sc2, gt2 = modulation(c_ctx, ada_w[l], ada_b[l])
        zero_state = (jnp.zeros((bp, 2, ML_HEADS, ML_HEAD_DIM, ML_HEAD_DIM), f32),
                      jnp.zeros((bp, 2, ML_HEADS, ML_HEAD_DIM), f32),
                      jnp.zeros((bp, 2, ML_HEADS), f32))
        mix, (cst, nst, mst), nk, nv = token_mixer(
            rmsnorm(xp, norm1_g[l]) * (1 + sc1) + sh1, zero_state, None, *lw)
        xp = xp + gt1 * mix
        xp = xp + gt2 * expert_choice_ffn(rmsnorm(xp, norm2_g[l]) * (1 + sc2) + sh2,
                                          w_router[l], w_gate[l], w_up[l], w_down[l])
        ks_l.append(nk)
        vs_l.append(nv)
        cs_l.append(cst.astype(xp.dtype))
        ns_l.append(nst.astype(xp.dtype))
        ms_l.append(mst.astype(xp.dtype))
        sh1, sc1, gt1, sh2, sc2, gt2 = [m[:, None, :] for m in modulation(c, ada_w[l], ada_b[l])]
        ml_state = (state_mlstm_c[:, l].astype(f32), state_mlstm_n[:, l].astype(f32),
                    state_mlstm_m[:, l].astype(f32))
        mix, _, _, _ = token_mixer(rmsnorm(xs, norm1_g[l]) * (1 + sc1) + sh1, ml_state,
                                   (cache_ctx_k[:, l], cache_ctx_v[:, l], na_rpb[l]), *lw)
        xs = xs + gt1 * mix
        xs = xs + gt2 * expert_choice_ffn(rmsnorm(xs, norm2_g[l]) * (1 + sc2) + sh2,
                                          w_router[l], w_gate[l], w_up[l], w_down[l])
    new_ctx_k = jnp.stack(ks_l, axis=1)
    new_ctx_v = jnp.stack(vs_l, axis=1)
    new_mlstm_c = jnp.stack(cs_l, axis=1)
    new_mlstm_n = jnp.stack(ns_l, axis=1)
    new_mlstm_m = jnp.stack(ms_l, axis=1)
    return (xp, xs, new_ctx_k, new_ctx_v, new_mlstm_c, new_mlstm_n, new_mlstm_m)
```

```python
import functools

import numpy as np
import jax
import jax.numpy as jnp
from jax import lax
from jax.experimental import pallas as pl
from jax.experimental.pallas import tpu as pltpu

F32 = jnp.float32
_MXU = jnp.bfloat16
_HI = lax.Precision.HIGHEST

GRID_W = 64
ML_HEADS = 4
ML_HEAD_DIM = 128
ML_WIDTH = ML_HEADS * ML_HEAD_DIM
CV_WIDTH = 512
CV_TAPS = 3
NA_HEADS = 8
NA_HEAD_DIM = 64
NA_WIDTH = NA_HEADS * NA_HEAD_DIM
NA_WIN_ROWS = 8
NA_WIN_COLS = 16
N_BRANCH = 3
N_EXPERTS = 16
EC_CAPACITY_FACTOR = 2
ROPE_BASE = 10000.0
NORM_EPS = 1e-6
PROJ_SIZES = (ML_WIDTH, ML_WIDTH, ML_WIDTH, ML_WIDTH, 2 * ML_HEADS, 2 * ML_HEADS,
              CV_WIDTH, CV_WIDTH, CV_WIDTH, NA_WIDTH, NA_WIDTH, NA_WIDTH)

LANES = 128
SUBLANES = 8
VMEM_LIMIT = 56 * 1024 * 1024
ML_CHUNK = 256
TOKEN_TILE = 256
NA_QROWS = 8
NA_QCOLS = 16
NA_KROWS = 16
NA_KBAND = 32
NEG = -1e30
BIG = 3.0e38


def _mm(a, b):
    return jnp.dot(a.astype(_MXU), b.astype(_MXU), preferred_element_type=F32)


def _mm_nt(a, b):
    return lax.dot_general(a.astype(_MXU), b.astype(_MXU), (((1,), (1,)), ((), ())),
                           preferred_element_type=F32)


def _mm_hi(a, b):
    return jnp.dot(a, b, precision=_HI, preferred_element_type=F32)


def _mm_split(a, b):
    hi = a.astype(_MXU)
    lo = (a - hi.astype(F32)).astype(_MXU)
    return (jnp.dot(hi, b, preferred_element_type=F32)
            + jnp.dot(lo, b, preferred_element_type=F32))


def _rms_mod(x, g, sc, sh):
    y = x * lax.rsqrt(jnp.mean(x * x, axis=-1, keepdims=True) + NORM_EPS) * g
    return y * (1.0 + sc) + sh


def _sigmoid(x):
    return 1.0 / (1.0 + jnp.exp(-x))


def _log_sigmoid(x):
    return -(jnp.maximum(-x, 0.0) + jnp.log(1.0 + jnp.exp(-jnp.abs(x))))


def _const_spec(shape):
    nd = len(shape)
    return pl.BlockSpec(shape, lambda *_: (0,) * nd, pipeline_mode=pl.Buffered(1))


def _params(sem):
    return pltpu.CompilerParams(dimension_semantics=sem, vmem_limit_bytes=VMEM_LIMIT)


def _mod_kernel(c_ref, w_ref, b_ref, o_ref):
    cv = c_ref[...]
    s = cv * _sigmoid(cv)
    o_ref[0] = _mm_hi(s, w_ref[0]) + b_ref[0]


def _modulation(cvec, ada_w, ada_b):
    depth, d, n6 = ada_w.shape
    tn = n6 // 4
    return pl.pallas_call(
        _mod_kernel,
        grid=(depth, n6 // tn),
        in_specs=[pl.BlockSpec(cvec.shape, lambda l, j: (0, 0)),
                  pl.BlockSpec((1, d, tn), lambda l, j: (l, 0, j)),
                  pl.BlockSpec((1, 1, tn), lambda l, j: (l, 0, j))],
        out_specs=pl.BlockSpec((1, cvec.shape[0], tn), lambda l, j: (l, 0, j)),
        out_shape=jax.ShapeDtypeStruct((depth, cvec.shape[0], n6), F32),
        compiler_params=_params(("arbitrary", "arbitrary")),
        name="modulation",
    )(cvec, ada_w, ada_b.reshape(depth, 1, n6))


def _rope(x, cos, sin):
    w = x.shape[1]
    lane = lax.broadcasted_iota(jnp.int32, (1, w), 1)
    first = (lane % 64) < 32
    partner = jnp.where(first, pltpu.roll(x, w - 32, 1), pltpu.roll(x, 32, 1))
    return x * cos + partner * sin


def _head_rms64(x, bd, g):
    ms = _mm_split(x * x, bd)
    return x * lax.rsqrt(ms + NORM_EPS) * g


def _in_kernel(latent, *refs):
    if latent:
        (x_ref, g1_ref, sc_ref, sh_ref, wa_ref, ba_ref, wg_ref, bg_ref, fb_ref, bd_ref, qg_ref,
         kg_ref, cos_ref, sin_ref, mq_ref, mk_ref, mv_ref, nq_ref, nk_ref, nv_ref, gate_ref) = refs
    else:
        (x_ref, g1_ref, sc_ref, sh_ref, wa_ref, ba_ref, wg_ref, bg_ref, fb_ref, bd_ref, qg_ref,
         kg_ref, mq_ref, mk_ref, mv_ref, nq_ref, nk_ref, nv_ref, gate_ref) = refs
    h = _rms_mod(x_ref[0], g1_ref[...], sc_ref[0], sh_ref[0]).astype(_MXU)

    def seg(j):
        sl = slice(j * 512, (j + 1) * 512)
        return jnp.dot(h, wa_ref[:, sl], preferred_element_type=F32) + ba_ref[:, sl]

    mq, mk = seg(0), seg(1)
    if latent:
        cos = jnp.concatenate([cos_ref[...]] * ML_HEADS, axis=1)
        sin = jnp.concatenate([sin_ref[...]] * ML_HEADS, axis=1)
        mq = _rope(mq, cos, sin)
        mk = _rope(mk, cos, sin)
    mq_ref[0] = mq.astype(mq_ref.dtype)
    mk_ref[0] = (mk * ML_HEAD_DIM ** -0.5).astype(mk_ref.dtype)
    mv_ref[0] = seg(2).astype(mv_ref.dtype)
    bd = bd_ref[...]
    nq_ref[0] = (_head_rms64(seg(3), bd, qg_ref[...]) * NA_HEAD_DIM ** -0.5).astype(nq_ref.dtype)
    nk_ref[0] = _head_rms64(seg(4), bd, kg_ref[...])
    nv_ref[0] = seg(5)
    pg = jnp.dot(h, wg_ref[...], preferred_element_type=F32) + bg_ref[...]
    lane = lax.broadcasted_iota(jnp.int32, (1, 2 * LANES), 1) % LANES
    is_f = (lane >= ML_HEADS) & (lane < 2 * ML_HEADS)
    pg = jnp.where(is_f, _log_sigmoid(pg + fb_ref[...]), pg)
    gate_ref[0, 0] = pg[:, :LANES]
    gate_ref[1, 0] = pg[:, LANES:]


def _in_proj(x, mod, lw, rope_tab):
    bsz, n, d = x.shape
    latent = rope_tab is not None
    tm = TOKEN_TILE
    sc, sh = mod
    per_batch = sc.shape[0] > 1
    midx = (lambda b, i: (b, 0, 0)) if per_batch else (lambda b, i: (0, 0, 0))
    tok = pl.BlockSpec((1, tm, 512), lambda b, i: (b, i, 0))
    in_specs = [pl.BlockSpec((1, tm, d), lambda b, i: (b, i, 0)),
                _const_spec((1, d)),
                pl.BlockSpec((1, 1, d), midx), pl.BlockSpec((1, 1, d), midx),
                _const_spec(lw["wa"].shape), _const_spec(lw["ba"].shape),
                _const_spec(lw["wg"].shape), _const_spec(lw["bg"].shape), _const_spec(lw["fb"].shape),
                _const_spec(lw["bd"].shape), _const_spec((1, 512)), _const_spec((1, 512))]
    args = [x, lw["norm1_g"], sc, sh, lw["wa"], lw["ba"], lw["wg"], lw["bg"], lw["fb"], lw["bd"],
            lw["na_q_g"], lw["na_k_g"]]
    if latent:
        in_specs += [pl.BlockSpec((tm, LANES), lambda b, i: (i, 0))] * 2
        args += list(rope_tab)
    act = jax.ShapeDtypeStruct((bsz, n, 512), _MXU)
    f32 = jax.ShapeDtypeStruct((bsz, n, 512), F32)
    return pl.pallas_call(
        functools.partial(_in_kernel, latent),
        grid=(bsz, n // tm),
        in_specs=in_specs,
        out_specs=[tok] * 6 + [pl.BlockSpec((2, 1, tm, LANES), lambda b, i: (0, b, i, 0))],
        out_shape=[act, act, act, act, f32, f32, jax.ShapeDtypeStruct((2, bsz, n, LANES), F32)],
        compiler_params=_params(("parallel", "parallel")),
        name="in_proj_latent" if latent else "in_proj_context",
    )(*args)


def _mlstm_kernel(want_state, q_ref, k_ref, v_ref, g_ref, c0_ref, n0_ref, m0_ref, *rest):
    if want_state:
        h_ref, cn_ref, nn_ref, mn_ref, c_s, n_s, m_s = rest
    else:
        h_ref, c_s, n_s, m_s = rest
    d = pl.program_id(1)
    c = pl.program_id(2)

    @pl.when(c == 0)
    def _():
        c_s[...] = c0_ref[0, 0]
        n_s[...] = n0_ref[0, 0]
        m_s[...] = m0_ref[0, 0]

    ln = q_ref.shape[1]
    row = lax.broadcasted_iota(jnp.int32, (ln, ln), 0)
    col = lax.broadcasted_iota(jnp.int32, (ln, ln), 1)
    sgn = 1 - 2 * d
    mask = (row - col) * sgn >= 0
    maskf = mask.astype(F32)
    mask_t = ((col - row) * sgn >= 0).astype(F32)
    g = g_ref[0, 0]
    g_t = g.T
    bcol_all = _mm_hi(maskf, g)
    brow_all = _mm_hi(g_t[0:SUBLANES], mask_t)
    btot_all = jnp.sum(g, axis=0, keepdims=True)
    q = q_ref[0]
    k = k_ref[0]
    v = v_ref[0]
    for h in range(ML_HEADS):
        sl = slice(h * ML_HEAD_DIM, (h + 1) * ML_HEAD_DIM)
        qh, kh, vh = q[:, sl], k[:, sl], v[:, sl]
        bcol = bcol_all[:, ML_HEADS + h:ML_HEADS + h + 1]
        igcol = g[:, h:h + 1]
        brow = brow_all[ML_HEADS + h:ML_HEADS + h + 1, :]
        igrow = g_t[h:h + 1, :]
        btot = btot_all[:, ML_HEADS + h:ML_HEADS + h + 1]
        m_prev = m_s[h][:, 0:1]
        c_prev = c_s[h]
        n_prev = n_s[h]
        dlog = jnp.where(mask, bcol - brow + igrow, -jnp.inf)
        a = bcol + m_prev
        m_t = jnp.maximum(a, jnp.max(dlog, axis=-1, keepdims=True))
        inter = jnp.exp(a - m_t)
        s = _mm_nt(qh, kh) * jnp.exp(dlog - m_t)
        num = inter * _mm(qh, c_prev) + _mm(s, vh)
        den = (inter * jnp.sum(qh.astype(F32) * n_prev, axis=-1, keepdims=True)
               + jnp.sum(s, axis=-1, keepdims=True))
        h_ref[0, 0, :, sl] = num / jnp.maximum(jnp.abs(den), jnp.exp(-m_t))
        m_new = jnp.maximum(btot + m_prev,
                            jnp.max(btot - brow + igrow, axis=-1, keepdims=True))
        w_end = jnp.exp(btot - bcol + igcol - m_new)
        decay = jnp.exp(btot + m_prev - m_new)
        kw = kh.astype(F32) * w_end
        c_s[h] = decay * c_prev + _mm(kw.T, vh)
        n_s[h] = decay * n_prev + jnp.sum(kw, axis=0, keepdims=True)
        m_s[h] = jnp.broadcast_to(m_new, (1, LANES))

    if want_state:
        @pl.when(c == pl.num_programs(2) - 1)
        def _():
            cn_ref[0, 0] = c_s[...]
            nn_ref[0, 0] = n_s[...]
            mn_ref[0, 0] = m_s[...]


def _mlstm(q, k, v, gates, c0, n0, m0, want_state):
    bsz, n, _ = q.shape
    ln = min(ML_CHUNK, n)
    nc = n // ln
    hd = ML_HEAD_DIM
    n0 = n0.reshape(bsz, 2, ML_HEADS, 1, hd)
    m0 = jnp.broadcast_to(m0[..., None, None], (bsz, 2, ML_HEADS, 1, LANES))

    def chunk(b, d, c):
        return c + d * (nc - 1 - 2 * c)

    tok = pl.BlockSpec((1, ln, 512), lambda b, d, c: (b, chunk(b, d, c), 0))
    st_c = pl.BlockSpec((1, 1, ML_HEADS, hd, hd), lambda b, d, c: (b, d, 0, 0, 0))
    st_n = pl.BlockSpec((1, 1, ML_HEADS, 1, hd), lambda b, d, c: (b, d, 0, 0, 0))
    st_m = pl.BlockSpec((1, 1, ML_HEADS, 1, LANES), lambda b, d, c: (b, d, 0, 0, 0))
    out_specs = [pl.BlockSpec((1, 1, ln, 512), lambda b, d, c: (d, b, chunk(b, d, c), 0))]
    out_shape = [jax.ShapeDtypeStruct((2, bsz, n, 512), F32)]
    if want_state:
        out_specs += [st_c, st_n, st_m]
        out_shape += [jax.ShapeDtypeStruct((bsz, 2, ML_HEADS, hd, hd), F32),
                      jax.ShapeDtypeStruct((bsz, 2, ML_HEADS, 1, hd), F32),
                      jax.ShapeDtypeStruct((bsz, 2, ML_HEADS, 1, LANES), F32)]
    return pl.pallas_call(
        functools.partial(_mlstm_kernel, want_state),
        grid=(bsz, 2, nc),
        in_specs=[tok, tok, tok,
                  pl.BlockSpec((1, 1, ln, LANES), lambda b, d, c: (d, b, chunk(b, d, c), 0)),
                  st_c, st_n, st_m],
        out_specs=out_specs,
        out_shape=out_shape,
        scratch_shapes=[pltpu.VMEM((ML_HEADS, hd, hd), F32),
                        pltpu.VMEM((ML_HEADS, 1, hd), F32),
                        pltpu.VMEM((ML_HEADS, 1, LANES), F32)],
        compiler_params=_params(("parallel", "parallel", "arbitrary")),
        name="mlstm_state" if want_state else "mlstm",
    )(q, k, v, gates, c0, n0, m0)


def _head_mask(hh):
    lane = lax.broadcasted_iota(jnp.int32, (1, LANES), 1)
    return (lane >= NA_HEAD_DIM * hh) & (lane < NA_HEAD_DIM * (hh + 1))


def _na_ctx_kernel(q_ref, k_ref, v_ref, y_ref):
    q = q_ref[0]
    k = k_ref[0]
    v = v_ref[0]
    for j in range(NA_WIDTH // LANES):
        sl = slice(j * LANES, (j + 1) * LANES)
        q2 = q[:, sl]
        k2 = k[:, sl].astype(_MXU)
        v2 = v[:, sl].astype(_MXU)
        y2 = None
        for hh in range(2):
            hm = _head_mask(hh)
            s = _mm_nt(jnp.where(hm, q2, jnp.zeros_like(q2)), k2)
            p = jnp.exp(s - jnp.max(s, axis=-1, keepdims=True))
            o = _mm(p, v2) / jnp.sum(p, axis=-1, keepdims=True)
            y2 = o if hh == 0 else jnp.where(hm, o, y2)
        y_ref[0, :, sl] = y2


def _na_context(nq, nk, nv):
    bsz, n, w = nq.shape
    spec = pl.BlockSpec((1, n, w), lambda b: (b, 0, 0))
    return pl.pallas_call(
        _na_ctx_kernel,
        grid=(bsz,),
        in_specs=[spec, spec, spec],
        out_specs=spec,
        out_shape=jax.ShapeDtypeStruct((bsz, n, w), F32),
        compiler_params=_params(("parallel",)),
        name="na_context",
    )(nq, nk, nv)


def _na_lat_kernel(q_ref, k0, k1, k2, k3, v0, v1, v2, v3, ck_ref, cv_ref, t_ref, mask_ref, y_ref):
    rb = pl.program_id(1)
    n = pl.program_id(2)
    rows = pl.num_programs(1) * NA_QROWS
    s0 = jnp.clip(NA_QROWS * rb - NA_WIN_ROWS // 2, 0, rows - NA_KROWS)
    off_r = NA_QROWS * rb - s0
    bs = pl.multiple_of(jnp.clip(NA_QCOLS * n - NA_WIN_COLS // 2, 0, GRID_W - NA_KBAND), SUBLANES)
    off_c = pl.multiple_of(NA_QCOLS * n - bs, SUBLANES)
    nq_tile = NA_QROWS * NA_QCOLS
    q = q_ref[0].reshape(nq_tile, NA_WIDTH)
    madd = mask_ref[0, 0]
    for j in range(NA_WIDTH // LANES):
        sl = slice(j * LANES, (j + 1) * LANES)
        q2 = q[:, sl]
        kwin = jnp.concatenate(
            [r[0, 0, :, pl.ds(bs, NA_KBAND), sl].reshape(4 * NA_KBAND, LANES) for r in (k0, k1, k2, k3)],
            axis=0).astype(_MXU)
        vwin = jnp.concatenate(
            [r[0, 0, :, pl.ds(bs, NA_KBAND), sl].reshape(4 * NA_KBAND, LANES) for r in (v0, v1, v2, v3)],
            axis=0).astype(_MXU)
        ck2 = ck_ref[0, :, sl].astype(_MXU)
        cv2 = cv_ref[0, :, sl].astype(_MXU)
        y2 = None
        for hh in range(2):
            hm = _head_mask(hh)
            qm = jnp.where(hm, q2, jnp.zeros_like(q2))
            bias = t_ref[2 * j + hh, pl.ds(off_r, NA_QROWS), pl.ds(off_c, NA_QCOLS), :]
            sw = _mm_nt(qm, kwin) + bias.reshape(nq_tile, NA_KROWS * NA_KBAND) + madd
            sc = _mm_nt(qm, ck2)
            m = jnp.maximum(jnp.max(sw, axis=-1, keepdims=True), jnp.max(sc, axis=-1, keepdims=True))
            pw = jnp.exp(sw - m)
            pc = jnp.exp(sc - m)
            den = jnp.sum(pw, axis=-1, keepdims=True) + jnp.sum(pc, axis=-1, keepdims=True)
            o = (_mm(pw, vwin) + _mm(pc, cv2)) / den
            y2 = o if hh == 0 else jnp.where(hm, o, y2)
        y_ref[0, :, :, sl] = y2.reshape(NA_QROWS, NA_QCOLS, LANES)


def _na_tables(rpb, rows):
    a = np.arange(NA_KROWS)
    cc = np.arange(NA_KBAND)
    dr = np.clip(a[None, :] - a[:, None] + NA_WIN_ROWS - 1, 0, 2 * NA_WIN_ROWS - 2)
    dc = np.clip(cc[None, :] - cc[:, None] + NA_WIN_COLS - 1, 0, 2 * NA_WIN_COLS - 2)
    oh_r = (dr[:, :, None] == np.arange(2 * NA_WIN_ROWS - 1)).astype(np.float32)
    oh_c = (dc[:, :, None] == np.arange(2 * NA_WIN_COLS - 1)).astype(np.float32)
    table = jnp.einsum("aki,hij->hakj", oh_r, rpb, precision=_HI)
    table = jnp.einsum("hakj,cmj->hackm", table, oh_c, precision=_HI)
    table = table.reshape(NA_HEADS, NA_KROWS, NA_KBAND, NA_KROWS * NA_KBAND).astype(F32)
    nrb = rows // NA_QROWS
    ncb = GRID_W // NA_QCOLS
    kh = min(NA_WIN_ROWS, rows)
    r0 = np.arange(nrb) * NA_QROWS
    s0 = np.clip(r0 - NA_WIN_ROWS // 2, 0, rows - NA_KROWS)
    qr = r0[:, None] + np.arange(NA_QROWS)[None, :]
    kr = s0[:, None] + np.arange(NA_KROWS)[None, :]
    lo = np.clip(qr - NA_WIN_ROWS // 2, 0, rows - kh)
    row_ok = (kr[:, None, :] >= lo[:, :, None]) & (kr[:, None, :] < lo[:, :, None] + kh)
    c0 = np.arange(ncb) * NA_QCOLS
    bs = np.clip(c0 - NA_WIN_COLS // 2, 0, GRID_W - NA_KBAND)
    qc = c0[:, None] + np.arange(NA_QCOLS)[None, :]
    kc = bs[:, None] + np.arange(NA_KBAND)[None, :]
    qs = np.clip(qc - NA_WIN_COLS // 2, 0, GRID_W - NA_WIN_COLS)
    col_ok = (kc[:, None, :] >= qs[:, :, None]) & (kc[:, None, :] < qs[:, :, None] + NA_WIN_COLS)
    ok = row_ok[:, None, :, None, :, None] & col_ok[None, :, None, :, None, :]
    mask = np.where(ok, 0.0, NEG).astype(np.float32)
    mask = mask.reshape(nrb, ncb, NA_QROWS * NA_QCOLS, NA_KROWS * NA_KBAND)
    return table, jnp.asarray(mask)


def _na_latent(nq, nk, nv, ck, cv, table, mask):
    bsz, n, w = nq.shape
    rows = n // GRID_W
    assert rows % NA_QROWS == 0 and rows >= NA_KROWS
    nrb = rows // NA_QROWS
    ncb = GRID_W // NA_QCOLS
    ngr = rows // 4
    past = ck.shape[1]
    q4 = nq.reshape(bsz, rows, GRID_W, w)
    k5 = nk.reshape(bsz, ngr, 4, GRID_W, w)
    v5 = nv.reshape(bsz, ngr, 4, GRID_W, w)

    def kv_spec(i):
        def idx(b, rb, c):
            g0 = jnp.clip(2 * rb - 1, 0, ngr - 4)
            return (b, g0 + i, 0, 0, 0)
        return pl.BlockSpec((1, 1, 4, GRID_W, w), idx)

    qspec = pl.BlockSpec((1, NA_QROWS, NA_QCOLS, w), lambda b, rb, c: (b, rb, c, 0))
    cspec = pl.BlockSpec((1, past, w), lambda b, rb, c: (b, 0, 0))
    y = pl.pallas_call(
        _na_lat_kernel,
        grid=(bsz, nrb, ncb),
        in_specs=[qspec] + [kv_spec(i) for i in range(4)] + [kv_spec(i) for i in range(4)]
        + [cspec, cspec, _const_spec(table.shape),
           pl.BlockSpec((1, 1) + mask.shape[2:], lambda b, rb, c: (rb, c, 0, 0))],
        out_specs=qspec,
        out_shape=jax.ShapeDtypeStruct((bsz, rows, GRID_W, w), F32),
        compiler_params=_params(("parallel", "parallel", "arbitrary")),
        name="na_latent",
    )(q4, k5, k5, k5, k5, v5, v5, v5, v5, ck, cv, table, mask)
    return y.reshape(bsz, n, w)


def _merge_kernel(x_ref, xp_ref, xn_ref, hf_ref, hb_ref, yna_ref, g1_ref, sc1_ref, sh1_ref, gt1_ref,
                  w2_ref, b2_ref, mlg_ref, cvw_ref, wpa_ref, wpb_ref, wpc_ref, wo_ref,
                  g2_ref, sc2_ref, sh2_ref, wr_ref, x1_ref, h2_ref, aff_ref, afft_ref):
    i = pl.program_id(1)
    last = pl.num_programs(1) - 1
    tm = x_ref.shape[1]
    d = x_ref.shape[2]
    x = x_ref[0]
    g1, sc1, sh1 = g1_ref[...], sc1_ref[0], sh1_ref[0]
    h = _rms_mod(x, g1, sc1, sh1)
    hp = _rms_mod(xp_ref[0, 0], g1, sc1, sh1)
    hn = _rms_mod(xn_ref[0, 0], g1, sc1, sh1)
    hb16 = h.astype(_MXU)
    hext = jnp.concatenate([hp, h, hn], axis=0).astype(_MXU)

    def proj(lhs, lo, hi):
        return jnp.dot(lhs, w2_ref[:, lo:hi], preferred_element_type=F32) + b2_ref[:, lo:hi]

    mo = proj(hb16, 0, 512)
    cb = proj(hb16, 512, 1024)
    cch = proj(hext, 1024, 2048)
    u = cch[:, :512] * cch[:, 512:]
    rowi = lax.broadcasted_iota(jnp.int32, (tm + 2 * SUBLANES, 1), 0)
    pad = ((rowi == SUBLANES - 1) & (i == 0)) | ((rowi == tm + SUBLANES) & (i == last))
    u = jnp.where(pad, 0.0, u)
    cvw = cvw_ref[...]
    conv = (u[SUBLANES - 1:SUBLANES - 1 + tm] * cvw[0:1]
            + u[SUBLANES:SUBLANES + tm] * cvw[1:2]
            + u[SUBLANES + 1:SUBLANES + 1 + tm] * cvw[2:3])
    y_cv = cb * conv
    hml = hf_ref[0, 0] + hb_ref[0, 0]
    parts = []
    for hh in range(ML_HEADS):
        seg = hml[:, hh * ML_HEAD_DIM:(hh + 1) * ML_HEAD_DIM]
        parts.append(seg * lax.rsqrt(jnp.mean(seg * seg, axis=-1, keepdims=True) + NORM_EPS))
    y_ml = jnp.concatenate(parts, axis=1) * mlg_ref[...] * _sigmoid(mo)
    merged = _sigmoid(proj(hb16, 2048, 2048 + d)) * _mm(y_ml, wpa_ref[...])
    merged = merged + _sigmoid(proj(hb16, 2048 + d, 2048 + 2 * d)) * _mm(y_cv, wpb_ref[...])
    merged = merged + _sigmoid(proj(hb16, 2048 + 2 * d, 2048 + 3 * d)) * _mm(yna_ref[0], wpc_ref[...])
    x1 = x + gt1_ref[0] * _mm(merged, wo_ref[...])
    x1_ref[0] = x1
    h2 = _rms_mod(x1, g2_ref[...], sc2_ref[0], sh2_ref[0])
    h2_ref[0] = h2
    logits = _mm_hi(h2, wr_ref[...])
    lane = lax.broadcasted_iota(jnp.int32, (1, LANES), 1)
    logits = jnp.where(lane < N_EXPERTS, logits, NEG)
    e = jnp.exp(logits - jnp.max(logits, axis=-1, keepdims=True))
    aff = e / jnp.sum(e, axis=-1, keepdims=True)
    aff_ref[0] = aff
    afft_ref[0] = aff.T[:N_EXPERTS]


def _merge(x, hdir, yna, mod, lw):
    bsz, n, d = x.shape
    tm = TOKEN_TILE
    sc1, sh1, gt1, sc2, sh2 = mod
    per_batch = sc1.shape[0] > 1
    midx = (lambda b, i: (b, 0, 0)) if per_batch else (lambda b, i: (0, 0, 0))
    mspec = pl.BlockSpec((1, 1, d), midx)
    x8 = x.reshape(bsz, n // SUBLANES, SUBLANES, d)
    nb8 = tm // SUBLANES
    tok_d = pl.BlockSpec((1, tm, d), lambda b, i: (b, i, 0))
    tok_w = pl.BlockSpec((1, tm, 512), lambda b, i: (b, i, 0))
    names = ["w2", "b2", "ml_out_g", "cv_w", "w_pa", "w_pb", "w_pc", "w_o"]
    in_specs = ([tok_d,
                 pl.BlockSpec((1, 1, SUBLANES, d), lambda b, i: (b, jnp.maximum(i * nb8 - 1, 0), 0, 0)),
                 pl.BlockSpec((1, 1, SUBLANES, d),
                              lambda b, i: (b, jnp.minimum((i + 1) * nb8, n // SUBLANES - 1), 0, 0)),
                 pl.BlockSpec((1, 1, tm, 512), lambda b, i: (0, b, i, 0)),
                 pl.BlockSpec((1, 1, tm, 512), lambda b, i: (1, b, i, 0)),
                 tok_w, _const_spec((1, d)), mspec, mspec, mspec]
                + [_const_spec(lw[k].shape) for k in names]
                + [_const_spec((1, d)), mspec, mspec, _const_spec(lw["w_r"].shape)])
    args = ([x, x8, x8, hdir, hdir, yna, lw["norm1_g"], sc1, sh1, gt1] + [lw[k] for k in names]
            + [lw["norm2_g"], sc2, sh2, lw["w_r"]])
    return pl.pallas_call(
        _merge_kernel,
        grid=(bsz, n // tm),
        in_specs=in_specs,
        out_specs=[tok_d, tok_d, pl.BlockSpec((1, tm, LANES), lambda b, i: (b, i, 0)),
                   pl.BlockSpec((1, N_EXPERTS, tm), lambda b, i: (b, 0, i))],
        out_shape=[jax.ShapeDtypeStruct((bsz, n, d), F32), jax.ShapeDtypeStruct((bsz, n, d), F32),
                   jax.ShapeDtypeStruct((bsz, n, LANES), F32),
                   jax.ShapeDtypeStruct((bsz, N_EXPERTS, n), F32)],
        compiler_params=_params(("parallel", "parallel")),
        name="merge",
    )(*args)


def _route_kernel(cap, aff_ref, idx_ref, incl_s):
    a = aff_ref[0]
    ne, n = a.shape

    def count(m):
        return jnp.sum(jnp.where(m, 1.0, 0.0), axis=1, keepdims=True)

    def cond(s):
        return (s[2] > 0) & (s[3] < 512)

    def body(s):
        lo, hi, _, it = s
        in_c = (a >= lo) & (a < hi)
        cmin = jnp.min(jnp.where(in_c, a, BIG), axis=1, keepdims=True)
        cmax = jnp.max(jnp.where(in_c, a, -BIG), axis=1, keepdims=True)
        done = cmin >= cmax
        mid = cmin + (cmax - cmin) * 0.5
        mid = jnp.where(mid > cmin, mid, cmax)
        ok = count(a >= mid) >= cap
        lo = jnp.where(done, lo, jnp.where(ok, mid, lo))
        hi = jnp.where(done, hi, jnp.where(ok, hi, mid))
        return lo, hi, jnp.sum(jnp.where(done, 0, 1)), it + 1

    lo, hi, _, _ = lax.while_loop(
        cond, body, (jnp.zeros((ne, 1), F32), jnp.full((ne, 1), BIG, F32), jnp.int32(1), jnp.int32(0)))
    gt = a >= hi
    eq = (a >= lo) & (a < hi)
    need = cap - count(gt)
    tok = lax.broadcasted_iota(jnp.int32, (ne, n), 1)

    def isearch(_, lh):
        jl, jh = lh
        mid = jl + ((jh - jl) >> 1)
        ok = count(eq & (tok < mid)) >= need
        return jnp.where(ok, jl, mid), jnp.where(ok, mid, jh)

    _, jmax = lax.fori_loop(0, int(np.ceil(np.log2(n))) + 1, isearch,
                            (jnp.zeros((ne, 1), jnp.int32), jnp.full((ne, 1), n, jnp.int32)))
    self = jnp.where(gt | (eq & (tok < jmax)), 1.0, 0.0)
    ch = min(256, n)
    tri = (lax.broadcasted_iota(jnp.int32, (ch, ch), 0)
           <= lax.broadcasted_iota(jnp.int32, (ch, ch), 1)).astype(_MXU)
    off = jnp.zeros((ne, 1), F32)
    for c in range(n // ch):
        inc = jnp.dot(self[:, c * ch:(c + 1) * ch].astype(_MXU), tri, preferred_element_type=F32) + off
        incl_s[:, c * ch:(c + 1) * ch] = inc
        off = inc[:, ch - 1:ch]
    tch = min(512, n)
    slot = lax.broadcasted_iota(jnp.int32, (cap, 1), 0).astype(F32)
    lane = lax.broadcasted_iota(jnp.int32, (1, LANES), 1)
    out = jnp.zeros((cap, LANES), F32)
    for e in range(ne):
        def chunk(c, acc):
            base = pl.multiple_of(c * tch, tch)
            row = incl_s[e:e + 1, pl.ds(base, tch)]
            hit = jnp.where(row <= slot, 1.0, 0.0)
            for k in range(tch // LANES):
                acc = acc + hit[:, k * LANES:(k + 1) * LANES]
            return acc
        acc = lax.fori_loop(0, n // tch, chunk, jnp.zeros((cap, LANES), F32))
        out = jnp.where(lane == e, jnp.sum(acc, axis=1, keepdims=True), out)
    idx_ref[0] = out.astype(jnp.int32)


def _route(aff_t, cap):
    bsz, ne, n = aff_t.shape
    idx_t = pl.pallas_call(
        functools.partial(_route_kernel, cap),
        grid=(bsz,),
        in_specs=[pl.BlockSpec((1, ne, n), lambda b: (b, 0, 0))],
        out_specs=pl.BlockSpec((1, cap, LANES), lambda b: (b, 0, 0)),
        out_shape=jax.ShapeDtypeStruct((bsz, cap, LANES), jnp.int32),
        scratch_shapes=[pltpu.VMEM((ne, n), F32)],
        compiler_params=_params(("parallel",)),
        name="route",
    )(aff_t)
    return idx_t[:, :, :ne].transpose(0, 2, 1)


def _ffn_kernel(idx_ref, h2_ref, aff_ref, wg_ref, wu_ref, wd_ref, out_ref, xg_s, ag_s, y_s, acc_s, sem):
    o = pl.program_id(0)
    e = pl.program_id(1)
    m = xg_s.shape[0]

    @pl.when(e == 0)
    def _():
        acc_s[...] = jnp.zeros_like(acc_s)

    def gather(r, carry):
        t = idx_ref[0, 0, 0, r]
        xg_s[pl.ds(r, 1), :] = h2_ref[0, pl.ds(t, 1), :]
        ag_s[pl.ds(r, 1), :] = aff_ref[0, pl.ds(t, 1), :]
        return carry

    lax.fori_loop(0, m, gather, 0, unroll=8)
    xg = xg_s[...].astype(_MXU)
    a = jnp.dot(xg, wg_ref[0], preferred_element_type=F32)
    u = jnp.dot(xg, wu_ref[0], preferred_element_type=F32)
    act = (a * _sigmoid(a) * u).astype(_MXU)
    lane = lax.broadcasted_iota(jnp.int32, (1, LANES), 1)
    gate = jnp.sum(jnp.where(lane == e, ag_s[...], 0.0), axis=1, keepdims=True)
    y_s[...] = jnp.dot(act, wd_ref[0], preferred_element_type=F32) * gate

    def scatter(r, carry):
        t = idx_ref[0, 0, 0, r]
        acc_s[pl.ds(t, 1), :] = acc_s[pl.ds(t, 1), :] + y_s[pl.ds(r, 1), :]
        return carry

    lax.fori_loop(0, m, scatter, 0, unroll=8)

    @pl.when(e == pl.num_programs(1) - 1)
    def _():
        cp = pltpu.make_async_copy(acc_s, out_ref.at[o], sem)
        cp.start()
        cp.wait()


def _ffn(h2, aff, idx, lw):
    outer, r, d = h2.shape
    m = idx.shape[2]
    ff = lw["w_gate"].shape[2]
    idx = idx.reshape(outer, N_EXPERTS, 1, m)
    return pl.pallas_call(
        _ffn_kernel,
        grid=(outer, N_EXPERTS),
        in_specs=[pl.BlockSpec((1, 1, 1, m), lambda o, e: (o, e, 0, 0), memory_space=pltpu.SMEM),
                  pl.BlockSpec((1, r, d), lambda o, e: (o, 0, 0), pipeline_mode=pl.Buffered(1)),
                  pl.BlockSpec((1, r, LANES), lambda o, e: (o, 0, 0), pipeline_mode=pl.Buffered(1)),
                  pl.BlockSpec((1, d, ff), lambda o, e: (e, 0, 0)),
                  pl.BlockSpec((1, d, ff), lambda o, e: (e, 0, 0)),
                  pl.BlockSpec((1, ff, d), lambda o, e: (e, 0, 0))],
        out_specs=pl.BlockSpec(memory_space=pl.ANY),
        out_shape=jax.ShapeDtypeStruct((outer, r, d), F32),
        scratch_shapes=[pltpu.VMEM((m, d), F32), pltpu.VMEM((m, LANES), F32), pltpu.VMEM((m, d), F32),
                        pltpu.VMEM((r, d), F32), pltpu.SemaphoreType.DMA(())],
        compiler_params=_params(("arbitrary", "arbitrary")),
        name="expert_ffn",
    )(idx, h2, aff, lw["w_gate"], lw["w_up"], lw["w_down"])


def _res_kernel(x_ref, a_ref, g_ref, o_ref):
    o_ref[0] = x_ref[0] + g_ref[0] * a_ref[0]


def _residual(x1, acc, gt2):
    bsz, n, d = x1.shape
    tm = min(1024, n)
    per_batch = gt2.shape[0] > 1
    midx = (lambda b, i: (b, 0, 0)) if per_batch else (lambda b, i: (0, 0, 0))
    tok = pl.BlockSpec((1, tm, d), lambda b, i: (b, i, 0))
    return pl.pallas_call(
        _res_kernel,
        grid=(bsz, n // tm),
        in_specs=[tok, tok, pl.BlockSpec((1, 1, d), midx)],
        out_specs=tok,
        out_shape=jax.ShapeDtypeStruct((bsz, n, d), F32),
        compiler_params=_params(("parallel", "parallel")),
        name="residual",
    )(x1, acc, gt2)


def _expert_ffn(x1, h2, aff, aff_t, gt2, lw, group_all):
    bsz, n, d = x1.shape
    cap = EC_CAPACITY_FACTOR * n // N_EXPERTS
    idx = _route(aff_t, cap)
    inner = bsz if group_all else 1
    outer = bsz // inner
    rows = (jnp.arange(inner, dtype=jnp.int32) * n)[None, :, None, None]
    idx = idx.reshape(outer, inner, N_EXPERTS, cap) + rows
    idx = idx.transpose(0, 2, 1, 3).reshape(outer, N_EXPERTS, inner * cap)
    acc = _ffn(h2.reshape(outer, inner * n, d), aff.reshape(outer, inner * n, LANES), idx, lw)
    return _residual(x1, acc.reshape(bsz, n, d), gt2)


def _rope_tables(n):
    t = jnp.arange(n)
    row = (t // GRID_W).astype(F32)
    col = (t % GRID_W).astype(F32)
    half = ML_HEAD_DIM // 4
    freqs = ROPE_BASE ** (-jnp.arange(half, dtype=F32) / half)
    ar = row[:, None] * freqs[None, :]
    ac = col[:, None] * freqs[None, :]
    cos = jnp.concatenate([jnp.cos(ar), jnp.cos(ar), jnp.cos(ac), jnp.cos(ac)], axis=1)
    sin = jnp.concatenate([-jnp.sin(ar), jnp.sin(ar), -jnp.sin(ac), jnp.sin(ac)], axis=1)
    return cos, sin


def _layer_weights(l, p):
    d = p["w_in"].shape[1]
    offs = np.concatenate([[0], np.cumsum(PROJ_SIZES)]).tolist()
    w_in, b_in = p["w_in"][l], p["b_in"][l]
    col = lambda i: w_in[:, offs[i]:offs[i + 1]]
    bcol = lambda i: b_in[offs[i]:offs[i + 1]]
    gt_w, gt_b = w_in[:, offs[12]:], b_in[offs[12]:]
    a_ids = (0, 1, 2, 9, 10, 11)
    wa = jnp.concatenate([col(i) for i in a_ids], axis=1).astype(_MXU)
    ba = jnp.concatenate([bcol(i) for i in a_ids])[None, :]
    w2 = jnp.concatenate([col(3), col(6), col(7), col(8), gt_w], axis=1).astype(_MXU)
    b2 = jnp.concatenate([bcol(3), bcol(6), bcol(7), bcol(8), gt_b])[None, :]
    mi_w, mf_w, mi_b, mf_b = col(4), col(5), bcol(4), bcol(5)
    h = ML_HEADS
    zw = jnp.zeros((d, LANES - 2 * h), F32)
    zb = jnp.zeros((LANES - 2 * h,), F32)
    wg = jnp.concatenate([mi_w[:, :h], mf_w[:, :h], zw, mi_w[:, h:], mf_w[:, h:], zw], axis=1).astype(_MXU)
    bg = jnp.concatenate([mi_b[:h], mf_b[:h], zb, mi_b[h:], mf_b[h:], zb])[None, :]
    zh = jnp.zeros((h,), F32)
    fbias = p["ml_fbias"][l]
    fb = jnp.concatenate([zh, fbias[0], zb, zh, fbias[1], zb])[None, :]
    hid = np.arange(NA_WIDTH) // NA_HEAD_DIM
    bd = jnp.asarray((hid[:, None] == hid[None, :]).astype(np.float32) / NA_HEAD_DIM).astype(_MXU)
    w_r = jnp.concatenate([p["w_router"][l], jnp.zeros((d, LANES - N_EXPERTS), F32)], axis=1)
    return dict(
        wa=wa, ba=ba, w2=w2, b2=b2, wg=wg, bg=bg, fb=fb, bd=bd, w_r=w_r,
        norm1_g=p["norm1_g"][l][None, :], norm2_g=p["norm2_g"][l][None, :],
        na_q_g=jnp.tile(p["na_q_g"][l], NA_HEADS)[None, :], na_k_g=jnp.tile(p["na_k_g"][l], NA_HEADS)[None, :],
        ml_out_g=p["ml_out_g"][l][None, :], cv_w=p["cv_w"][l],
        w_pa=p["w_pa"][l].astype(_MXU), w_pb=p["w_pb"][l].astype(_MXU), w_pc=p["w_pc"][l].astype(_MXU),
        w_o=p["w_o"][l].astype(_MXU),
        w_gate=p["w_gate"][l].astype(_MXU), w_up=p["w_up"][l].astype(_MXU),
        w_down=p["w_down"][l].astype(_MXU))


def _sublayers(x, mod6, lw, ml_state, na_ctx, rope_tab, want_state, group_all):
    sh1, sc1, gt1, sh2, sc2, gt2 = mod6
    mq, mk, mv, nq, nk, nv, gates = _in_proj(x, (sc1, sh1), lw, rope_tab)
    ml = _mlstm(mq, mk, mv, gates, *ml_state, want_state)
    if na_ctx is None:
        yna = _na_context(nq, nk, nv)
    else:
        yna = _na_latent(nq, nk, nv, *na_ctx)
    x1, h2, aff, aff_t = _merge(x, ml[0], yna, (sc1, sh1, gt1, sc2, sh2), lw)
    x2 = _expert_ffn(x1, h2, aff, aff_t, gt2, lw, group_all)
    return x2, ml[1:], nk, nv


def kernel(x_prompt, x_sample, c, cache_ctx_k, cache_ctx_v, state_mlstm_c, state_mlstm_n, state_mlstm_m,
           c_ctx, ada_w, ada_b, norm1_g, norm2_g, w_in, b_in, ml_fbias, ml_out_g, cv_w, na_q_g, na_k_g,
           na_rpb, w_pa, w_pb, w_pc, w_o, w_router, w_gate, w_up, w_down):
    p = dict(w_in=w_in, b_in=b_in, ml_fbias=ml_fbias, ml_out_g=ml_out_g, cv_w=cv_w, na_q_g=na_q_g,
             na_k_g=na_k_g, w_pa=w_pa, w_pb=w_pb, w_pc=w_pc, w_o=w_o, w_router=w_router, w_gate=w_gate,
             w_up=w_up, w_down=w_down, norm1_g=norm1_g, norm2_g=norm2_g)
    depth = ada_w.shape[0]
    bp, np_tok, d = x_prompt.shape
    bs, ns_tok, _ = x_sample.shape
    hd = ML_HEAD_DIM
    nrow = -(-(bs + 1) // SUBLANES) * SUBLANES
    cvec = jnp.zeros((nrow, d), F32).at[:bs].set(c).at[bs].set(c_ctx)
    mod = _modulation(cvec, ada_w, ada_b)
    rope_tab = _rope_tables(ns_tok)
    zero_state = (jnp.zeros((bp, 2, ML_HEADS, hd, hd), F32), jnp.zeros((bp, 2, ML_HEADS, hd), F32),
                  jnp.zeros((bp, 2, ML_HEADS), F32))
    xp, xs = x_prompt, x_sample
    ks_l, vs_l, cs_l, ns_l, ms_l = [], [], [], [], []
    for l in range(depth):
        lw = _layer_weights(l, p)
        m6 = mod[l].reshape(nrow, 6, d)
        mod_p = [m6[bs:bs + 1, i][:, None, :] for i in range(6)]
        mod_s = [m6[:bs, i][:, None, :] for i in range(6)]
        xp, (cst, nst, mst), nk, nv = _sublayers(xp, mod_p, lw, zero_state, None, None, True, True)
        ks_l.append(nk.reshape(bp, np_tok, NA_HEADS, NA_HEAD_DIM))
        vs_l.append(nv.reshape(bp, np_tok, NA_HEADS, NA_HEAD_DIM))
        cs_l.append(cst)
        ns_l.append(nst[:, :, :, 0, :])
        ms_l.append(mst[:, :, :, 0, 0])
        table, mask = _na_tables(na_rpb[l], ns_tok // GRID_W)
        past = cache_ctx_k.shape[2]
        na_ctx = (cache_ctx_k[:, l].reshape(bs, past, NA_WIDTH), cache_ctx_v[:, l].reshape(bs, past, NA_WIDTH),
                  table, mask)
        ml_state = (state_mlstm_c[:, l], state_mlstm_n[:, l], state_mlstm_m[:, l])
        xs, _, _, _ = _sublayers(xs, mod_s, lw, ml_state, na_ctx, rope_tab, False, False)
    return (xp, xs, jnp.stack(ks_l, axis=1), jnp.stack(vs_l, axis=1), jnp.stack(cs_l, axis=1),
            jnp.stack(ns_l, axis=1), jnp.stack(ms_l, axis=1))
```

```python
import functools

import numpy as np
import jax
import jax.numpy as jnp
from jax import lax
from jax.experimental import pallas as pl
from jax.experimental.pallas import tpu as pltpu

F32 = jnp.float32
_MXU = jnp.bfloat16
_HI = lax.Precision.HIGHEST

GRID_W = 64
ML_HEADS = 4
ML_HEAD_DIM = 128
ML_WIDTH = ML_HEADS * ML_HEAD_DIM
CV_WIDTH = 512
CV_TAPS = 3
NA_HEADS = 8
NA_HEAD_DIM = 64
NA_WIDTH = NA_HEADS * NA_HEAD_DIM
NA_WIN_ROWS = 8
NA_WIN_COLS = 16
N_BRANCH = 3
N_EXPERTS = 16
EC_CAPACITY_FACTOR = 2
ROPE_BASE = 10000.0
NORM_EPS = 1e-6
PROJ_SIZES = (ML_WIDTH, ML_WIDTH, ML_WIDTH, ML_WIDTH, 2 * ML_HEADS, 2 * ML_HEADS,
              CV_WIDTH, CV_WIDTH, CV_WIDTH, NA_WIDTH, NA_WIDTH, NA_WIDTH)

LANES = 128
SUBLANES = 8
VMEM_LIMIT = 56 * 1024 * 1024
ML_CHUNK = 256
TOKEN_TILE = 512
NA_QROWS = 8
NA_QCOLS = 16
NA_KROWS = 16
NA_KBAND = 32
FFN_SCATTER_GROUP = 4
FFN_PIECES = 4
NEG = -1e30
BIG = 3.0e38


def _mm(a, b):
    return jnp.dot(a.astype(_MXU), b.astype(_MXU), preferred_element_type=F32)


def _mm_nt(a, b):
    return lax.dot_general(a.astype(_MXU), b.astype(_MXU), (((1,), (1,)), ((), ())),
                           preferred_element_type=F32)


def _mm_hi(a, b):
    return jnp.dot(a, b, precision=_HI, preferred_element_type=F32)


def _split3(x):
    hi = x.astype(_MXU)
    r = x - hi.astype(F32)
    mid = r.astype(_MXU)
    lo = (r - mid.astype(F32)).astype(_MXU)
    return hi, mid, lo


def _mm_split(a, b):
    hi = a.astype(_MXU)
    lo = (a - hi.astype(F32)).astype(_MXU)
    return (jnp.dot(hi, b, preferred_element_type=F32)
            + jnp.dot(lo, b, preferred_element_type=F32))


def _mm_exact_l(a, b):
    return sum(jnp.dot(t, b, preferred_element_type=F32) for t in _split3(a))


def _mm_exact_r(a, b):
    return sum(jnp.dot(a, t, preferred_element_type=F32) for t in _split3(b))


def _rms_mod(x, g, sc, sh):
    y = x * lax.rsqrt(jnp.mean(x * x, axis=-1, keepdims=True) + NORM_EPS) * g
    return y * (1.0 + sc) + sh


def _sigmoid(x):
    return 1.0 / (1.0 + jnp.exp(-x))


def _log_sigmoid(x):
    return -(jnp.maximum(-x, 0.0) + jnp.log(1.0 + jnp.exp(-jnp.abs(x))))


def _const_spec(shape):
    nd = len(shape)
    return pl.BlockSpec(shape, lambda *_: (0,) * nd, pipeline_mode=pl.Buffered(1))


def _params(sem):
    return pltpu.CompilerParams(dimension_semantics=sem, vmem_limit_bytes=VMEM_LIMIT)


def _mod_kernel(c_ref, w_ref, b_ref, o_ref):
    cv = c_ref[...]
    s = cv * _sigmoid(cv)
    o_ref[0] = _mm_hi(s, w_ref[0]) + b_ref[0]


def _modulation(cvec, ada_w, ada_b):
    depth, d, n6 = ada_w.shape
    tn = n6 // 4
    return pl.pallas_call(
        _mod_kernel,
        grid=(depth, n6 // tn),
        in_specs=[pl.BlockSpec(cvec.shape, lambda l, j: (0, 0)),
                  pl.BlockSpec((1, d, tn), lambda l, j: (l, 0, j)),
                  pl.BlockSpec((1, 1, tn), lambda l, j: (l, 0, j))],
        out_specs=pl.BlockSpec((1, cvec.shape[0], tn), lambda l, j: (l, 0, j)),
        out_shape=jax.ShapeDtypeStruct((depth, cvec.shape[0], n6), F32),
        compiler_params=_params(("arbitrary", "arbitrary")),
        name="modulation",
    )(cvec, ada_w, ada_b.reshape(depth, 1, n6))


def _rope(x, cos, sin):
    w = x.shape[1]
    lane = lax.broadcasted_iota(jnp.int32, (1, w), 1)
    first = (lane % 64) < 32
    partner = jnp.where(first, pltpu.roll(x, w - 32, 1), pltpu.roll(x, 32, 1))
    return x * cos + partner * sin


def _head_rms64(x, bd, g):
    ms = _mm_split(x * x, bd)
    return x * lax.rsqrt(ms + NORM_EPS) * g


def _in_kernel(latent, *refs):
    if latent:
        (x_ref, g1_ref, sc_ref, sh_ref, wa_ref, ba_ref, wg_ref, bg_ref, fb_ref, bd_ref, qg_ref,
         kg_ref, cos_ref, sin_ref, mq_ref, mk_ref, mv_ref, nq_ref, nk_ref, nv_ref, gate_ref) = refs
    else:
        (x_ref, g1_ref, sc_ref, sh_ref, wa_ref, ba_ref, wg_ref, bg_ref, fb_ref, bd_ref, qg_ref,
         kg_ref, mq_ref, mk_ref, mv_ref, nq_ref, nk_ref, nv_ref, gate_ref) = refs
    h = _rms_mod(x_ref[0], g1_ref[...], sc_ref[0], sh_ref[0]).astype(_MXU)

    def seg(j):
        sl = slice(j * 512, (j + 1) * 512)
        return jnp.dot(h, wa_ref[:, sl], preferred_element_type=F32) + ba_ref[:, sl]

    mq, mk = seg(0), seg(1)
    if latent:
        cos = jnp.concatenate([cos_ref[...]] * ML_HEADS, axis=1)
        sin = jnp.concatenate([sin_ref[...]] * ML_HEADS, axis=1)
        mq = _rope(mq, cos, sin)
        mk = _rope(mk, cos, sin)
    mq_ref[0] = mq.astype(mq_ref.dtype)
    mk_ref[0] = (mk * ML_HEAD_DIM ** -0.5).astype(mk_ref.dtype)
    mv_ref[0] = seg(2).astype(mv_ref.dtype)
    bd = bd_ref[...]
    nq_ref[0] = (_head_rms64(seg(3), bd, qg_ref[...]) * NA_HEAD_DIM ** -0.5).astype(nq_ref.dtype)
    nk_ref[0] = _head_rms64(seg(4), bd, kg_ref[...])
    nv_ref[0] = seg(5)
    pg = jnp.dot(h, wg_ref[...], preferred_element_type=F32) + bg_ref[...]
    lane = lax.broadcasted_iota(jnp.int32, (1, 2 * LANES), 1) % LANES
    is_f = (lane >= ML_HEADS) & (lane < 2 * ML_HEADS)
    pg = jnp.where(is_f, _log_sigmoid(pg + fb_ref[...]), pg)
    gate_ref[0, 0] = pg[:, :LANES]
    gate_ref[1, 0] = pg[:, LANES:]


def _in_proj(x, mod, lw, rope_tab):
    bsz, n, d = x.shape
    latent = rope_tab is not None
    tm = min(TOKEN_TILE, n)
    sc, sh = mod
    per_batch = sc.shape[0] > 1
    midx = (lambda b, i: (b, 0, 0)) if per_batch else (lambda b, i: (0, 0, 0))
    tok = pl.BlockSpec((1, tm, 512), lambda b, i: (b, i, 0))
    in_specs = [pl.BlockSpec((1, tm, d), lambda b, i: (b, i, 0)),
                _const_spec((1, d)),
                pl.BlockSpec((1, 1, d), midx), pl.BlockSpec((1, 1, d), midx),
                _const_spec(lw["wa"].shape), _const_spec(lw["ba"].shape),
                _const_spec(lw["wg"].shape), _const_spec(lw["bg"].shape), _const_spec(lw["fb"].shape),
                _const_spec(lw["bd"].shape), _const_spec((1, 512)), _const_spec((1, 512))]
    args = [x, lw["norm1_g"], sc, sh, lw["wa"], lw["ba"], lw["wg"], lw["bg"], lw["fb"], lw["bd"],
            lw["na_q_g"], lw["na_k_g"]]
    if latent:
        in_specs += [pl.BlockSpec((tm, LANES), lambda b, i: (i, 0))] * 2
        args += list(rope_tab)
    act = jax.ShapeDtypeStruct((bsz, n, 512), _MXU)
    f32 = jax.ShapeDtypeStruct((bsz, n, 512), F32)
    return pl.pallas_call(
        functools.partial(_in_kernel, latent),
        grid=(bsz, n // tm),
        in_specs=in_specs,
        out_specs=[tok] * 6 + [pl.BlockSpec((2, 1, tm, LANES), lambda b, i: (0, b, i, 0))],
        out_shape=[act, act, act, act, f32, f32, jax.ShapeDtypeStruct((2, bsz, n, LANES), F32)],
        compiler_params=_params(("parallel", "parallel")),
        name="in_proj_latent" if latent else "in_proj_context",
    )(*args)


def _mlstm_kernel(want_state, q_ref, k_ref, v_ref, g_ref, c0_ref, n0_ref, m0_ref, *rest):
    if want_state:
        h_ref, cn_ref, nn_ref, mn_ref, c_s, n_s, m_s = rest
    else:
        h_ref, c_s, n_s, m_s = rest
    d = pl.program_id(1)
    c = pl.program_id(2)

    @pl.when(c == 0)
    def _():
        c_s[...] = c0_ref[0, 0]
        n_s[...] = n0_ref[0, 0]
        m_s[...] = m0_ref[0, 0]

    ln = q_ref.shape[1]
    row = lax.broadcasted_iota(jnp.int32, (ln, ln), 0)
    col = lax.broadcasted_iota(jnp.int32, (ln, ln), 1)
    sgn = 1 - 2 * d
    mask = (row - col) * sgn >= 0
    maskf = mask.astype(_MXU)
    mask_t = (col - row) * sgn >= 0
    g = g_ref[0, 0]
    g_t = g.T
    bcol_all = _mm_exact_r(maskf, g)
    brow_all = _mm_exact_l(g_t[0:SUBLANES], mask_t.astype(_MXU))
    btot_all = jnp.sum(g, axis=0, keepdims=True)
    q = q_ref[0]
    k = k_ref[0]
    v = v_ref[0]
    for h in range(ML_HEADS):
        sl = slice(h * ML_HEAD_DIM, (h + 1) * ML_HEAD_DIM)
        qh, kh, vh = q[:, sl], k[:, sl], v[:, sl]
        bcol = bcol_all[:, ML_HEADS + h:ML_HEADS + h + 1]
        igcol = g[:, h:h + 1]
        brow = brow_all[ML_HEADS + h:ML_HEADS + h + 1, :]
        igrow = g_t[h:h + 1, :]
        btot = btot_all[:, ML_HEADS + h:ML_HEADS + h + 1]
        m_prev = m_s[h][:, 0:1]
        c_prev = c_s[h]
        n_prev = n_s[h]
        dlog = jnp.where(mask, bcol - brow + igrow, -jnp.inf)
        a = bcol + m_prev
        m_t = jnp.maximum(a, jnp.max(dlog, axis=-1, keepdims=True))
        inter = jnp.exp(a - m_t)
        s = _mm_nt(qh, kh) * jnp.exp(dlog - m_t)
        num = inter * _mm(qh, c_prev) + _mm(s, vh)
        den = (inter * jnp.sum(qh.astype(F32) * n_prev, axis=-1, keepdims=True)
               + jnp.sum(s, axis=-1, keepdims=True))
        h_ref[0, 0, :, sl] = (num / jnp.maximum(jnp.abs(den), jnp.exp(-m_t))).astype(h_ref.dtype)
        m_new = jnp.maximum(btot + m_prev,
                            jnp.max(btot - brow + igrow, axis=-1, keepdims=True))
        w_end = jnp.exp(btot - bcol + igcol - m_new)
        decay = jnp.exp(btot + m_prev - m_new)
        kw = kh.astype(F32) * w_end
        c_s[h] = decay * c_prev + _mm(kw.T, vh)
        n_s[h] = decay * n_prev + jnp.sum(kw, axis=0, keepdims=True)
        m_s[h] = jnp.broadcast_to(m_new, (1, LANES))

    if want_state:
        @pl.when(c == pl.num_programs(2) - 1)
        def _():
            cn_ref[0, 0] = c_s[...]
            nn_ref[0, 0] = n_s[...]
            mn_ref[0, 0] = m_s[...]


def _mlstm(q, k, v, gates, c0, n0, m0, want_state):
    bsz, n, _ = q.shape
    ln = min(ML_CHUNK, n)
    nc = n // ln
    hd = ML_HEAD_DIM
    n0 = n0.reshape(bsz, 2, ML_HEADS, 1, hd)
    m0 = jnp.broadcast_to(m0[..., None, None], (bsz, 2, ML_HEADS, 1, LANES))

    def chunk(b, d, c):
        return c + d * (nc - 1 - 2 * c)

    tok = pl.BlockSpec((1, ln, 512), lambda b, d, c: (b, chunk(b, d, c), 0))
    st_c = pl.BlockSpec((1, 1, ML_HEADS, hd, hd), lambda b, d, c: (b, d, 0, 0, 0))
    st_n = pl.BlockSpec((1, 1, ML_HEADS, 1, hd), lambda b, d, c: (b, d, 0, 0, 0))
    st_m = pl.BlockSpec((1, 1, ML_HEADS, 1, LANES), lambda b, d, c: (b, d, 0, 0, 0))
    out_specs = [pl.BlockSpec((1, 1, ln, 512), lambda b, d, c: (d, b, chunk(b, d, c), 0))]
    out_shape = [jax.ShapeDtypeStruct((2, bsz, n, 512), _MXU)]
    if want_state:
        out_specs += [st_c, st_n, st_m]
        out_shape += [jax.ShapeDtypeStruct((bsz, 2, ML_HEADS, hd, hd), F32),
                      jax.ShapeDtypeStruct((bsz, 2, ML_HEADS, 1, hd), F32),
                      jax.ShapeDtypeStruct((bsz, 2, ML_HEADS, 1, LANES), F32)]
    return pl.pallas_call(
        functools.partial(_mlstm_kernel, want_state),
        grid=(bsz, 2, nc),
        in_specs=[tok, tok, tok,
                  pl.BlockSpec((1, 1, ln, LANES), lambda b, d, c: (d, b, chunk(b, d, c), 0)),
                  st_c, st_n, st_m],
        out_specs=out_specs,
        out_shape=out_shape,
        scratch_shapes=[pltpu.VMEM((ML_HEADS, hd, hd), F32),
                        pltpu.VMEM((ML_HEADS, 1, hd), F32),
                        pltpu.VMEM((ML_HEADS, 1, LANES), F32)],
        compiler_params=_params(("parallel", "parallel", "arbitrary")),
        name="mlstm_state" if want_state else "mlstm",
    )(q, k, v, gates, c0, n0, m0)


def _head_mask(hh):
    lane = lax.broadcasted_iota(jnp.int32, (1, LANES), 1)
    return (lane >= NA_HEAD_DIM * hh) & (lane < NA_HEAD_DIM * (hh + 1))


def _na_ctx_kernel(q_ref, k_ref, v_ref, y_ref):
    q = q_ref[0]
    k = k_ref[0]
    v = v_ref[0]
    for j in range(NA_WIDTH // LANES):
        sl = slice(j * LANES, (j + 1) * LANES)
        q2 = q[:, sl]
        k2 = k[:, sl].astype(_MXU)
        v2 = v[:, sl].astype(_MXU)
        y2 = None
        for hh in range(2):
            hm = _head_mask(hh)
            s = _mm_nt(jnp.where(hm, q2, jnp.zeros_like(q2)), k2)
            p = jnp.exp(s - jnp.max(s, axis=-1, keepdims=True))
            o = _mm(p, v2) / jnp.sum(p, axis=-1, keepdims=True)
            y2 = o if hh == 0 else jnp.where(hm, o, y2)
        y_ref[0, :, sl] = y2


def _na_context(nq, nk, nv):
    bsz, n, w = nq.shape
    spec = pl.BlockSpec((1, n, w), lambda b: (b, 0, 0))
    return pl.pallas_call(
        _na_ctx_kernel,
        grid=(bsz,),
        in_specs=[spec, spec, spec],
        out_specs=spec,
        out_shape=jax.ShapeDtypeStruct((bsz, n, w), F32),
        compiler_params=_params(("parallel",)),
        name="na_context",
    )(nq, nk, nv)


def _na_lat_kernel(q_ref, k0, k1, k2, k3, v0, v1, v2, v3, ck_ref, cv_ref, t_ref, mask_ref, y_ref):
    rb = pl.program_id(1)
    n = pl.program_id(2)
    rows = pl.num_programs(1) * NA_QROWS
    s0 = jnp.clip(NA_QROWS * rb - NA_WIN_ROWS // 2, 0, rows - NA_KROWS)
    off_r = NA_QROWS * rb - s0
    bs = pl.multiple_of(jnp.clip(NA_QCOLS * n - NA_WIN_COLS // 2, 0, GRID_W - NA_KBAND), SUBLANES)
    off_c = pl.multiple_of(NA_QCOLS * n - bs, SUBLANES)
    nq_tile = NA_QROWS * NA_QCOLS
    q = q_ref[0].reshape(nq_tile, NA_WIDTH)
    madd = mask_ref[0, 0]
    for j in range(NA_WIDTH // LANES):
        sl = slice(j * LANES, (j + 1) * LANES)
        q2 = q[:, sl]
        kwin = jnp.concatenate(
            [r[0, 0, :, pl.ds(bs, NA_KBAND), sl].reshape(4 * NA_KBAND, LANES) for r in (k0, k1, k2, k3)],
            axis=0).astype(_MXU)
        vwin = jnp.concatenate(
            [r[0, 0, :, pl.ds(bs, NA_KBAND), sl].reshape(4 * NA_KBAND, LANES) for r in (v0, v1, v2, v3)],
            axis=0).astype(_MXU)
        ck2 = ck_ref[0, :, sl].astype(_MXU)
        cv2 = cv_ref[0, :, sl].astype(_MXU)
        y2 = None
        for hh in range(2):
            hm = _head_mask(hh)
            qm = jnp.where(hm, q2, jnp.zeros_like(q2))
            bias = t_ref[2 * j + hh, pl.ds(off_r, NA_QROWS), pl.ds(off_c, NA_QCOLS), :]
            sw = _mm_nt(qm, kwin) + bias.reshape(nq_tile, NA_KROWS * NA_KBAND) + madd
            sc = _mm_nt(qm, ck2)
            m = jnp.maximum(jnp.max(sw, axis=-1, keepdims=True), jnp.max(sc, axis=-1, keepdims=True))
            pw = jnp.exp(sw - m)
            pc = jnp.exp(sc - m)
            den = jnp.sum(pw, axis=-1, keepdims=True) + jnp.sum(pc, axis=-1, keepdims=True)
            o = (_mm(pw, vwin) + _mm(pc, cv2)) / den
            y2 = o if hh == 0 else jnp.where(hm, o, y2)
        y_ref[0, :, :, sl] = y2.reshape(NA_QROWS, NA_QCOLS, LANES)


def _na_tables(rpb, rows):
    a = np.arange(NA_KROWS)
    cc = np.arange(NA_KBAND)
    dr = np.clip(a[None, :] - a[:, None] + NA_WIN_ROWS - 1, 0, 2 * NA_WIN_ROWS - 2)
    dc = np.clip(cc[None, :] - cc[:, None] + NA_WIN_COLS - 1, 0, 2 * NA_WIN_COLS - 2)
    oh_r = (dr[:, :, None] == np.arange(2 * NA_WIN_ROWS - 1)).astype(np.float32)
    oh_c = (dc[:, :, None] == np.arange(2 * NA_WIN_COLS - 1)).astype(np.float32)
    table = jnp.einsum("aki,hij->hakj", oh_r, rpb, precision=_HI)
    table = jnp.einsum("hakj,cmj->hackm", table, oh_c, precision=_HI)
    table = table.reshape(NA_HEADS, NA_KROWS, NA_KBAND, NA_KROWS * NA_KBAND).astype(F32)
    nrb = rows // NA_QROWS
    ncb = GRID_W // NA_QCOLS
    kh = min(NA_WIN_ROWS, rows)
    r0 = np.arange(nrb) * NA_QROWS
    s0 = np.clip(r0 - NA_WIN_ROWS // 2, 0, rows - NA_KROWS)
    qr = r0[:, None] + np.arange(NA_QROWS)[None, :]
    kr = s0[:, None] + np.arange(NA_KROWS)[None, :]
    lo = np.clip(qr - NA_WIN_ROWS // 2, 0, rows - kh)
    row_ok = (kr[:, None, :] >= lo[:, :, None]) & (kr[:, None, :] < lo[:, :, None] + kh)
    c0 = np.arange(ncb) * NA_QCOLS
    bs = np.clip(c0 - NA_WIN_COLS // 2, 0, GRID_W - NA_KBAND)
    qc = c0[:, None] + np.arange(NA_QCOLS)[None, :]
    kc = bs[:, None] + np.arange(NA_KBAND)[None, :]
    qs = np.clip(qc - NA_WIN_COLS // 2, 0, GRID_W - NA_WIN_COLS)
    col_ok = (kc[:, None, :] >= qs[:, :, None]) & (kc[:, None, :] < qs[:, :, None] + NA_WIN_COLS)
    ok = row_ok[:, None, :, None, :, None] & col_ok[None, :, None, :, None, :]
    mask = np.where(ok, 0.0, NEG).astype(np.float32)
    mask = mask.reshape(nrb, ncb, NA_QROWS * NA_QCOLS, NA_KROWS * NA_KBAND)
    return table, jnp.asarray(mask)


def _na_latent(nq, nk, nv, ck, cv, table, mask):
    bsz, n, w = nq.shape
    rows = n // GRID_W
    assert rows % NA_QROWS == 0 and rows >= NA_KROWS
    nrb = rows // NA_QROWS
    ncb = GRID_W // NA_QCOLS
    ngr = rows // 4
    past = ck.shape[1]
    q4 = nq.reshape(bsz, rows, GRID_W, w)
    k5 = nk.reshape(bsz, ngr, 4, GRID_W, w)
    v5 = nv.reshape(bsz, ngr, 4, GRID_W, w)

    def kv_spec(i):
        def idx(b, rb, c):
            g0 = jnp.clip(2 * rb - 1, 0, ngr - 4)
            return (b, g0 + i, 0, 0, 0)
        return pl.BlockSpec((1, 1, 4, GRID_W, w), idx)

    qspec = pl.BlockSpec((1, NA_QROWS, NA_QCOLS, w), lambda b, rb, c: (b, rb, c, 0))
    cspec = pl.BlockSpec((1, past, w), lambda b, rb, c: (b, 0, 0))
    y = pl.pallas_call(
        _na_lat_kernel,
        grid=(bsz, nrb, ncb),
        in_specs=[qspec] + [kv_spec(i) for i in range(4)] + [kv_spec(i) for i in range(4)]
        + [cspec, cspec, _const_spec(table.shape),
           pl.BlockSpec((1, 1) + mask.shape[2:], lambda b, rb, c: (rb, c, 0, 0))],
        out_specs=qspec,
        out_shape=jax.ShapeDtypeStruct((bsz, rows, GRID_W, w), F32),
        compiler_params=_params(("parallel", "parallel", "arbitrary")),
        name="na_latent",
    )(q4, k5, k5, k5, k5, v5, v5, v5, v5, ck, cv, table, mask)
    return y.reshape(bsz, n, w)


def _merge_kernel(x_ref, xp_ref, xn_ref, hf_ref, hb_ref, yna_ref, g1_ref, sc1_ref, sh1_ref, gt1_ref,
                  w2_ref, b2_ref, mlg_ref, cvw_ref, wpa_ref, wpb_ref, wpc_ref, wo_ref,
                  g2_ref, sc2_ref, sh2_ref, wr_ref, x1_ref, h2_ref, aff_ref, afft_ref):
    i = pl.program_id(1)
    last = pl.num_programs(1) - 1
    tm = x_ref.shape[1]
    d = x_ref.shape[2]
    x = x_ref[0]
    g1, sc1, sh1 = g1_ref[...], sc1_ref[0], sh1_ref[0]
    h = _rms_mod(x, g1, sc1, sh1)
    hp = _rms_mod(xp_ref[0, 0], g1, sc1, sh1)
    hn = _rms_mod(xn_ref[0, 0], g1, sc1, sh1)
    hb16 = h.astype(_MXU)
    hext = jnp.concatenate([hp, h, hn], axis=0).astype(_MXU)

    def proj(lhs, lo, hi):
        return jnp.dot(lhs, w2_ref[:, lo:hi], preferred_element_type=F32) + b2_ref[:, lo:hi]

    mo = proj(hb16, 0, 512)
    cb = proj(hb16, 512, 1024)
    cch = proj(hext, 1024, 2048)
    u = cch[:, :512] * cch[:, 512:]
    rowi = lax.broadcasted_iota(jnp.int32, (tm + 2 * SUBLANES, 1), 0)
    pad = ((rowi == SUBLANES - 1) & (i == 0)) | ((rowi == tm + SUBLANES) & (i == last))
    u = jnp.where(pad, 0.0, u)
    cvw = cvw_ref[...]
    conv = (u[SUBLANES - 1:SUBLANES - 1 + tm] * cvw[0:1]
            + u[SUBLANES:SUBLANES + tm] * cvw[1:2]
            + u[SUBLANES + 1:SUBLANES + 1 + tm] * cvw[2:3])
    y_cv = cb * conv
    hml = hf_ref[0, 0].astype(F32) + hb_ref[0, 0].astype(F32)
    parts = []
    for hh in range(ML_HEADS):
        seg = hml[:, hh * ML_HEAD_DIM:(hh + 1) * ML_HEAD_DIM]
        parts.append(seg * lax.rsqrt(jnp.mean(seg * seg, axis=-1, keepdims=True) + NORM_EPS))
    y_ml = jnp.concatenate(parts, axis=1) * mlg_ref[...] * _sigmoid(mo)
    merged = _sigmoid(proj(hb16, 2048, 2048 + d)) * _mm(y_ml, wpa_ref[...])
    merged = merged + _sigmoid(proj(hb16, 2048 + d, 2048 + 2 * d)) * _mm(y_cv, wpb_ref[...])
    merged = merged + _sigmoid(proj(hb16, 2048 + 2 * d, 2048 + 3 * d)) * _mm(yna_ref[0], wpc_ref[...])
    x1 = x + gt1_ref[0] * _mm(merged, wo_ref[...])
    x1_ref[0] = x1
    h2 = _rms_mod(x1, g2_ref[...], sc2_ref[0], sh2_ref[0])
    h2_ref[0] = h2
    logits = _mm(h2, wr_ref[...])
    lane = lax.broadcasted_iota(jnp.int32, (1, LANES), 1)
    logits = jnp.where(lane < N_EXPERTS, logits, NEG)
    e = jnp.exp(logits - jnp.max(logits, axis=-1, keepdims=True))
    aff = e / jnp.sum(e, axis=-1, keepdims=True)
    aff_ref[0] = aff
    afft_ref[0] = aff.T[:N_EXPERTS]


def _merge(x, hdir, yna, mod, lw):
    bsz, n, d = x.shape
    tm = min(TOKEN_TILE, n)
    sc1, sh1, gt1, sc2, sh2 = mod
    per_batch = sc1.shape[0] > 1
    midx = (lambda b, i: (b, 0, 0)) if per_batch else (lambda b, i: (0, 0, 0))
    mspec = pl.BlockSpec((1, 1, d), midx)
    x8 = x.reshape(bsz, n // SUBLANES, SUBLANES, d)
    nb8 = tm // SUBLANES
    tok_d = pl.BlockSpec((1, tm, d), lambda b, i: (b, i, 0))
    tok_w = pl.BlockSpec((1, tm, 512), lambda b, i: (b, i, 0))
    names = ["w2", "b2", "ml_out_g", "cv_w", "w_pa", "w_pb", "w_pc", "w_o"]
    in_specs = ([tok_d,
                 pl.BlockSpec((1, 1, SUBLANES, d), lambda b, i: (b, jnp.maximum(i * nb8 - 1, 0), 0, 0)),
                 pl.BlockSpec((1, 1, SUBLANES, d),
                              lambda b, i: (b, jnp.minimum((i + 1) * nb8, n // SUBLANES - 1), 0, 0)),
                 pl.BlockSpec((1, 1, tm, 512), lambda b, i: (0, b, i, 0)),
                 pl.BlockSpec((1, 1, tm, 512), lambda b, i: (1, b, i, 0)),
                 tok_w, _const_spec((1, d)), mspec, mspec, mspec]
                + [_const_spec(lw[k].shape) for k in names]
                + [_const_spec((1, d)), mspec, mspec, _const_spec(lw["w_r"].shape)])
    args = ([x, x8, x8, hdir, hdir, yna, lw["norm1_g"], sc1, sh1, gt1] + [lw[k] for k in names]
            + [lw["norm2_g"], sc2, sh2, lw["w_r"]])
    return pl.pallas_call(
        _merge_kernel,
        grid=(bsz, n // tm),
        in_specs=in_specs,
        out_specs=[tok_d, tok_d, pl.BlockSpec((1, tm, LANES), lambda b, i: (b, i, 0)),
                   pl.BlockSpec((1, N_EXPERTS, tm), lambda b, i: (b, 0, i))],
        out_shape=[jax.ShapeDtypeStruct((bsz, n, d), F32), jax.ShapeDtypeStruct((bsz, n, d), F32),
                   jax.ShapeDtypeStruct((bsz, n, LANES), F32),
                   jax.ShapeDtypeStruct((bsz, N_EXPERTS, n), F32)],
        compiler_params=_params(("parallel", "parallel")),
        name="merge",
    )(*args)


def _route_kernel(cap, aff_ref, idx_ref, incl_s):
    a = aff_ref[0]
    ne, n = a.shape

    def count(m):
        return jnp.sum(jnp.where(m, 1.0, 0.0), axis=1, keepdims=True)

    def cond(s):
        return (s[2] > 0) & (s[3] < 512)

    def body(s):
        lo, hi, _, it = s
        in_c = (a >= lo) & (a < hi)
        cmin = jnp.min(jnp.where(in_c, a, BIG), axis=1, keepdims=True)
        cmax = jnp.max(jnp.where(in_c, a, -BIG), axis=1, keepdims=True)
        done = cmin >= cmax
        mid = cmin + (cmax - cmin) * 0.5
        mid = jnp.where(mid > cmin, mid, cmax)
        ok = count(a >= mid) >= cap
        lo = jnp.where(done, lo, jnp.where(ok, mid, lo))
        hi = jnp.where(done, hi, jnp.where(ok, hi, mid))
        return lo, hi, jnp.sum(jnp.where(done, 0, 1)), it + 1

    lo, hi, _, _ = lax.while_loop(
        cond, body, (jnp.zeros((ne, 1), F32), jnp.full((ne, 1), BIG, F32), jnp.int32(1), jnp.int32(0)))
    gt = a >= hi
    eq = (a >= lo) & (a < hi)
    need = cap - count(gt)
    tok = lax.broadcasted_iota(jnp.int32, (ne, n), 1)

    def isearch(_, lh):
        jl, jh = lh
        mid = jl + ((jh - jl) >> 1)
        ok = count(eq & (tok < mid)) >= need
        return jnp.where(ok, jl, mid), jnp.where(ok, mid, jh)

    _, jmax = lax.fori_loop(0, int(np.ceil(np.log2(n))) + 1, isearch,
                            (jnp.zeros((ne, 1), jnp.int32), jnp.full((ne, 1), n, jnp.int32)))
    self = jnp.where(gt | (eq & (tok < jmax)), 1.0, 0.0)
    ch = min(256, n)
    tri = (lax.broadcasted_iota(jnp.int32, (ch, ch), 0)
           <= lax.broadcasted_iota(jnp.int32, (ch, ch), 1)).astype(_MXU)
    off = jnp.zeros((ne, 1), F32)
    for c in range(n // ch):
        inc = jnp.dot(self[:, c * ch:(c + 1) * ch].astype(_MXU), tri, preferred_element_type=F32) + off
        incl_s[:, c * ch:(c + 1) * ch] = inc
        off = inc[:, ch - 1:ch]
    tch = min(512, n)
    slot = lax.broadcasted_iota(jnp.int32, (cap, 1), 0).astype(F32)
    lane = lax.broadcasted_iota(jnp.int32, (1, LANES), 1)
    out = jnp.zeros((cap, LANES), F32)
    for e in range(ne):
        def chunk(c, acc):
            base = pl.multiple_of(c * tch, tch)
            row = incl_s[e:e + 1, pl.ds(base, tch)]
            hit = jnp.where(row <= slot, 1.0, 0.0)
            for k in range(tch // LANES):
                acc = acc + hit[:, k * LANES:(k + 1) * LANES]
            return acc
        acc = lax.fori_loop(0, n // tch, chunk, jnp.zeros((cap, LANES), F32))
        out = jnp.where(lane == e, jnp.sum(acc, axis=1, keepdims=True), out)
    idx_ref[0] = out.astype(jnp.int32)


def _route(aff_t, cap):
    bsz, ne, n = aff_t.shape
    idx_t = pl.pallas_call(
        functools.partial(_route_kernel, cap),
        grid=(bsz,),
        in_specs=[pl.BlockSpec((1, ne, n), lambda b: (b, 0, 0))],
        out_specs=pl.BlockSpec((1, cap, LANES), lambda b: (b, 0, 0)),
        out_shape=jax.ShapeDtypeStruct((bsz, cap, LANES), jnp.int32),
        scratch_shapes=[pltpu.VMEM((ne, n), F32)],
        compiler_params=_params(("parallel",)),
        name="route",
    )(aff_t)
    return idx_t[:, :, :ne].transpose(0, 2, 1)


def _ffn_kernel(idx_ref, h2_ref, aff_ref, wg_ref, wu_ref, wd_ref, out_ref, xg_s, ag_s, y_s, acc_s, sem):
    o = pl.program_id(0)
    e = pl.program_id(1)
    last = pl.num_programs(1) - 1
    m, d = xg_s.shape
    ff = wg_ref.shape[2]

    def gather_row(base, r):
        t = idx_ref[0, 0, base + r]
        xg_s[pl.ds(r, 1), :] = h2_ref[0, pl.ds(t, 1), :]
        ag_s[pl.ds(r, 1), :] = aff_ref[0, pl.ds(t, 1), :]

    def scatter_rows(base, r0, nrow):
        ts = [idx_ref[0, 0, base + r0 + k] for k in range(nrow)]
        vals = [acc_s[pl.ds(ts[k], 1), :] + y_s[pl.ds(r0 + k, 1), :] for k in range(nrow)]
        for k in range(nrow):
            acc_s[pl.ds(ts[k], 1), :] = vals[k]

    @pl.when(e == 0)
    def _():
        acc_s[...] = jnp.zeros_like(acc_s)
        y_s[...] = jnp.zeros_like(y_s)

        def body(r, carry):
            gather_row(0, r)
            return carry
        lax.fori_loop(0, m, body, 0, unroll=8)

    lane = lax.broadcasted_iota(jnp.int32, (1, LANES), 1)
    gate = jnp.sum(jnp.where(lane == e, ag_s[...], 0.0), axis=1, keepdims=True)
    xg = xg_s[...].astype(_MXU)
    prev = jnp.maximum(e - 1, 0) * m
    pieces = []
    npc = FFN_PIECES
    wcol = ff // npc
    rows_a = m // (2 * npc)
    r = 0
    for w_ref in (wg_ref, wu_ref):
        for j in range(npc):
            pieces.append(jnp.dot(xg, w_ref[0, :, j * wcol:(j + 1) * wcol], preferred_element_type=F32))
            for _ in range(rows_a // FFN_SCATTER_GROUP):
                scatter_rows(prev, r, FFN_SCATTER_GROUP)
                r += FFN_SCATTER_GROUP
    a = jnp.concatenate(pieces[:npc], axis=1)
    u = jnp.concatenate(pieces[npc:], axis=1)
    act = (a * _sigmoid(a) * u).astype(_MXU)
    nxt = jnp.minimum(e + 1, last) * m
    dcol = d // npc
    rows_b = m // npc
    r = 0
    for j in range(npc):
        sl = slice(j * dcol, (j + 1) * dcol)
        y_s[:, sl] = jnp.dot(act, wd_ref[0, :, sl], preferred_element_type=F32) * gate
        for _ in range(rows_b):
            gather_row(nxt, r)
            r += 1

    @pl.when(e == last)
    def _():
        def body(g, carry):
            scatter_rows(last * m, g * FFN_SCATTER_GROUP, FFN_SCATTER_GROUP)
            return carry
        lax.fori_loop(0, m // FFN_SCATTER_GROUP, body, 0, unroll=2)
        cp = pltpu.make_async_copy(acc_s, out_ref.at[o], sem)
        cp.start()
        cp.wait()


def _ffn(h2, aff, idx, lw):
    outer, r, d = h2.shape
    m = idx.shape[2]
    ff = lw["w_gate"].shape[2]
    idx = idx.reshape(outer, 1, N_EXPERTS * m)
    return pl.pallas_call(
        _ffn_kernel,
        grid=(outer, N_EXPERTS),
        in_specs=[pl.BlockSpec((1, 1, N_EXPERTS * m), lambda o, e: (o, 0, 0), memory_space=pltpu.SMEM),
                  pl.BlockSpec((1, r, d), lambda o, e: (o, 0, 0), pipeline_mode=pl.Buffered(1)),
                  pl.BlockSpec((1, r, LANES), lambda o, e: (o, 0, 0), pipeline_mode=pl.Buffered(1)),
                  pl.BlockSpec((1, d, ff), lambda o, e: (e, 0, 0)),
                  pl.BlockSpec((1, d, ff), lambda o, e: (e, 0, 0)),
                  pl.BlockSpec((1, ff, d), lambda o, e: (e, 0, 0))],
        out_specs=pl.BlockSpec(memory_space=pl.ANY),
        out_shape=jax.ShapeDtypeStruct((outer, r, d), F32),
        scratch_shapes=[pltpu.VMEM((m, d), F32), pltpu.VMEM((m, LANES), F32), pltpu.VMEM((m, d), F32),
                        pltpu.VMEM((r, d), F32), pltpu.SemaphoreType.DMA(())],
        compiler_params=_params(("arbitrary", "arbitrary")),
        name="expert_ffn",
    )(idx, h2, aff, lw["w_gate"], lw["w_up"], lw["w_down"])


def _res_kernel(x_ref, a_ref, g_ref, o_ref):
    o_ref[0] = x_ref[0] + g_ref[0] * a_ref[0]


def _residual(x1, acc, gt2):
    bsz, n, d = x1.shape
    tm = min(1024, n)
    per_batch = gt2.shape[0] > 1
    midx = (lambda b, i: (b, 0, 0)) if per_batch else (lambda b, i: (0, 0, 0))
    tok = pl.BlockSpec((1, tm, d), lambda b, i: (b, i, 0))
    return pl.pallas_call(
        _res_kernel,
        grid=(bsz, n // tm),
        in_specs=[tok, tok, pl.BlockSpec((1, 1, d), midx)],
        out_specs=tok,
        out_shape=jax.ShapeDtypeStruct((bsz, n, d), F32),
        compiler_params=_params(("parallel", "parallel")),
        name="residual",
    )(x1, acc, gt2)


def _expert_ffn(x1, h2, aff, aff_t, gt2, lw, group_all):
    bsz, n, d = x1.shape
    cap = EC_CAPACITY_FACTOR * n // N_EXPERTS
    idx = _route(aff_t, cap)
    inner = bsz if group_all else 1
    outer = bsz // inner
    rows = (jnp.arange(inner, dtype=jnp.int32) * n)[None, :, None, None]
    idx = idx.reshape(outer, inner, N_EXPERTS, cap) + rows
    idx = idx.transpose(0, 2, 1, 3).reshape(outer, N_EXPERTS, inner * cap)
    acc = _ffn(h2.reshape(outer, inner * n, d), aff.reshape(outer, inner * n, LANES), idx, lw)
    return _residual(x1, acc.reshape(bsz, n, d), gt2)


def _rope_tables(n):
    t = jnp.arange(n)
    row = (t // GRID_W).astype(F32)
    col = (t % GRID_W).astype(F32)
    half = ML_HEAD_DIM // 4
    freqs = ROPE_BASE ** (-jnp.arange(half, dtype=F32) / half)
    ar = row[:, None] * freqs[None, :]
    ac = col[:, None] * freqs[None, :]
    cos = jnp.concatenate([jnp.cos(ar), jnp.cos(ar), jnp.cos(ac), jnp.cos(ac)], axis=1)
    sin = jnp.concatenate([-jnp.sin(ar), jnp.sin(ar), -jnp.sin(ac), jnp.sin(ac)], axis=1)
    return cos, sin


def _layer_weights(l, p):
    d = p["w_in"].shape[1]
    offs = np.concatenate([[0], np.cumsum(PROJ_SIZES)]).tolist()
    w_in, b_in = p["w_in"][l], p["b_in"][l]
    col = lambda i: w_in[:, offs[i]:offs[i + 1]]
    bcol = lambda i: b_in[offs[i]:offs[i + 1]]
    gt_w, gt_b = w_in[:, offs[12]:], b_in[offs[12]:]
    a_ids = (0, 1, 2, 9, 10, 11)
    wa = jnp.concatenate([col(i) for i in a_ids], axis=1).astype(_MXU)
    ba = jnp.concatenate([bcol(i) for i in a_ids])[None, :]
    w2 = jnp.concatenate([col(3), col(6), col(7), col(8), gt_w], axis=1).astype(_MXU)
    b2 = jnp.concatenate([bcol(3), bcol(6), bcol(7), bcol(8), gt_b])[None, :]
    mi_w, mf_w, mi_b, mf_b = col(4), col(5), bcol(4), bcol(5)
    h = ML_HEADS
    zw = jnp.zeros((d, LANES - 2 * h), F32)
    zb = jnp.zeros((LANES - 2 * h,), F32)
    wg = jnp.concatenate([mi_w[:, :h], mf_w[:, :h], zw, mi_w[:, h:], mf_w[:, h:], zw], axis=1).astype(_MXU)
    bg = jnp.concatenate([mi_b[:h], mf_b[:h], zb, mi_b[h:], mf_b[h:], zb])[None, :]
    zh = jnp.zeros((h,), F32)
    fbias = p["ml_fbias"][l]
    fb = jnp.concatenate([zh, fbias[0], zb, zh, fbias[1], zb])[None, :]
    hid = np.arange(NA_WIDTH) // NA_HEAD_DIM
    bd = jnp.asarray((hid[:, None] == hid[None, :]).astype(np.float32) / NA_HEAD_DIM).astype(_MXU)
    w_r = jnp.concatenate([p["w_router"][l], jnp.zeros((d, LANES - N_EXPERTS), F32)], axis=1)
    return dict(
        wa=wa, ba=ba, w2=w2, b2=b2, wg=wg, bg=bg, fb=fb, bd=bd, w_r=w_r,
        norm1_g=p["norm1_g"][l][None, :], norm2_g=p["norm2_g"][l][None, :],
        na_q_g=jnp.tile(p["na_q_g"][l], NA_HEADS)[None, :], na_k_g=jnp.tile(p["na_k_g"][l], NA_HEADS)[None, :],
        ml_out_g=p["ml_out_g"][l][None, :], cv_w=p["cv_w"][l],
        w_pa=p["w_pa"][l].astype(_MXU), w_pb=p["w_pb"][l].astype(_MXU), w_pc=p["w_pc"][l].astype(_MXU),
        w_o=p["w_o"][l].astype(_MXU),
        w_gate=p["w_gate"][l].astype(_MXU), w_up=p["w_up"][l].astype(_MXU),
        w_down=p["w_down"][l].astype(_MXU))


def _sublayers(x, mod6, lw, ml_state, na_ctx, rope_tab, want_state, group_all):
    sh1, sc1, gt1, sh2, sc2, gt2 = mod6
    mq, mk, mv, nq, nk, nv, gates = _in_proj(x, (sc1, sh1), lw, rope_tab)
    ml = _mlstm(mq, mk, mv, gates, *ml_state, want_state)
    if na_ctx is None:
        yna = _na_context(nq, nk, nv)
    else:
        yna = _na_latent(nq, nk, nv, *na_ctx)
    x1, h2, aff, aff_t = _merge(x, ml[0], yna, (sc1, sh1, gt1, sc2, sh2), lw)
    x2 = _expert_ffn(x1, h2, aff, aff_t, gt2, lw, group_all)
    return x2, ml[1:], nk, nv


def kernel(x_prompt, x_sample, c, cache_ctx_k, cache_ctx_v, state_mlstm_c, state_mlstm_n, state_mlstm_m,
           c_ctx, ada_w, ada_b, norm1_g, norm2_g, w_in, b_in, ml_fbias, ml_out_g, cv_w, na_q_g, na_k_g,
           na_rpb, w_pa, w_pb, w_pc, w_o, w_router, w_gate, w_up, w_down):
    p = dict(w_in=w_in, b_in=b_in, ml_fbias=ml_fbias, ml_out_g=ml_out_g, cv_w=cv_w, na_q_g=na_q_g,
             na_k_g=na_k_g, w_pa=w_pa, w_pb=w_pb, w_pc=w_pc, w_o=w_o, w_router=w_router, w_gate=w_gate,
             w_up=w_up, w_down=w_down, norm1_g=norm1_g, norm2_g=norm2_g)
    depth = ada_w.shape[0]
    bp, np_tok, d = x_prompt.shape
    bs, ns_tok, _ = x_sample.shape
    hd = ML_HEAD_DIM
    nrow = -(-(bs + 1) // SUBLANES) * SUBLANES
    cvec = jnp.zeros((nrow, d), F32).at[:bs].set(c).at[bs].set(c_ctx)
    mod = _modulation(cvec, ada_w, ada_b)
    rope_tab = _rope_tables(ns_tok)
    zero_state = (jnp.zeros((bp, 2, ML_HEADS, hd, hd), F32), jnp.zeros((bp, 2, ML_HEADS, hd), F32),
                  jnp.zeros((bp, 2, ML_HEADS), F32))
    xp, xs = x_prompt, x_sample
    ks_l, vs_l, cs_l, ns_l, ms_l = [], [], [], [], []
    for l in range(depth):
        lw = _layer_weights(l, p)
        m6 = mod[l].reshape(nrow, 6, d)
        mod_p = [m6[bs:bs + 1, i][:, None, :] for i in range(6)]
        mod_s = [m6[:bs, i][:, None, :] for i in range(6)]
        xp, (cst, nst, mst), nk, nv = _sublayers(xp, mod_p, lw, zero_state, None, None, True, True)
        ks_l.append(nk.reshape(bp, np_tok, NA_HEADS, NA_HEAD_DIM))
        vs_l.append(nv.reshape(bp, np_tok, NA_HEADS, NA_HEAD_DIM))
        cs_l.append(cst)
        ns_l.append(nst[:, :, :, 0, :])
        ms_l.append(mst[:, :, :, 0, 0])
        table, mask = _na_tables(na_rpb[l], ns_tok // GRID_W)
        past = cache_ctx_k.shape[2]
        na_ctx = (cache_ctx_k[:, l].reshape(bs, past, NA_WIDTH).astype(_MXU),
                  cache_ctx_v[:, l].reshape(bs, past, NA_WIDTH).astype(_MXU), table, mask)
        ml_state = (state_mlstm_c[:, l], state_mlstm_n[:, l], state_mlstm_m[:, l])
        xs, _, _, _ = _sublayers(xs, mod_s, lw, ml_state, na_ctx, rope_tab, False, False)
    return (xp, xs, jnp.stack(ks_l, axis=1), jnp.stack(vs_l, axis=1), jnp.stack(cs_l, axis=1),
            jnp.stack(ns_l, axis=1), jnp.stack(ms_l, axis=1))
```

```python
import functools

import numpy as np
import jax
import jax.numpy as jnp
from jax import lax
from jax.experimental import pallas as pl
from jax.experimental.pallas import tpu as pltpu

F32 = jnp.float32
_MXU = jnp.bfloat16
_HI = lax.Precision.HIGHEST

GRID_W = 64
ML_HEADS = 4
ML_HEAD_DIM = 128
ML_WIDTH = ML_HEADS * ML_HEAD_DIM
CV_WIDTH = 512
CV_TAPS = 3
NA_HEADS = 8
NA_HEAD_DIM = 64
NA_WIDTH = NA_HEADS * NA_HEAD_DIM
NA_WIN_ROWS = 8
NA_WIN_COLS = 16
N_BRANCH = 3
N_EXPERTS = 16
EC_CAPACITY_FACTOR = 2
ROPE_BASE = 10000.0
NORM_EPS = 1e-6
PROJ_SIZES = (ML_WIDTH, ML_WIDTH, ML_WIDTH, ML_WIDTH, 2 * ML_HEADS, 2 * ML_HEADS,
              CV_WIDTH, CV_WIDTH, CV_WIDTH, NA_WIDTH, NA_WIDTH, NA_WIDTH)

LANES = 128
SUBLANES = 8
VMEM_LIMIT = 56 * 1024 * 1024
ML_CHUNK = 256
TOKEN_TILE = 512
NA_QROWS = 8
NA_QCOLS = 16
NA_KROWS = 16
NA_KBAND = 32
FFN_SCATTER_GROUP = 4
FFN_PIECES = 4
NEG = -1e30
BIG = 3.0e38


def _mm(a, b):
    return jnp.dot(a.astype(_MXU), b.astype(_MXU), preferred_element_type=F32)


def _mm_nt(a, b):
    return lax.dot_general(a.astype(_MXU), b.astype(_MXU), (((1,), (1,)), ((), ())),
                           preferred_element_type=F32)


def _mm_hi(a, b):
    return jnp.dot(a, b, precision=_HI, preferred_element_type=F32)


def _split3(x):
    hi = x.astype(_MXU)
    r = x - hi.astype(F32)
    mid = r.astype(_MXU)
    lo = (r - mid.astype(F32)).astype(_MXU)
    return hi, mid, lo


def _mm_split(a, b):
    hi = a.astype(_MXU)
    lo = (a - hi.astype(F32)).astype(_MXU)
    return (jnp.dot(hi, b, preferred_element_type=F32)
            + jnp.dot(lo, b, preferred_element_type=F32))


def _mm_exact_l(a, b):
    return sum(jnp.dot(t, b, preferred_element_type=F32) for t in _split3(a))


def _mm_exact_r(a, b):
    return sum(jnp.dot(a, t, preferred_element_type=F32) for t in _split3(b))


def _rms_mod(x, g, sc, sh):
    y = x * lax.rsqrt(jnp.mean(x * x, axis=-1, keepdims=True) + NORM_EPS) * g
    return y * (1.0 + sc) + sh


def _sigmoid(x):
    return 1.0 / (1.0 + jnp.exp(-x))


def _log_sigmoid(x):
    return -(jnp.maximum(-x, 0.0) + jnp.log(1.0 + jnp.exp(-jnp.abs(x))))


def _const_spec(shape):
    nd = len(shape)
    return pl.BlockSpec(shape, lambda *_: (0,) * nd, pipeline_mode=pl.Buffered(1))


def _params(sem):
    return pltpu.CompilerParams(dimension_semantics=sem, vmem_limit_bytes=VMEM_LIMIT)


def _mod_kernel(c_ref, w_ref, b_ref, o_ref):
    cv = c_ref[...]
    s = cv * _sigmoid(cv)
    o_ref[0] = _mm_hi(s, w_ref[0]) + b_ref[0]


def _modulation(cvec, ada_w, ada_b):
    depth, d, n6 = ada_w.shape
    tn = n6 // 4
    return pl.pallas_call(
        _mod_kernel,
        grid=(depth, n6 // tn),
        in_specs=[pl.BlockSpec(cvec.shape, lambda l, j: (0, 0)),
                  pl.BlockSpec((1, d, tn), lambda l, j: (l, 0, j)),
                  pl.BlockSpec((1, 1, tn), lambda l, j: (l, 0, j))],
        out_specs=pl.BlockSpec((1, cvec.shape[0], tn), lambda l, j: (l, 0, j)),
        out_shape=jax.ShapeDtypeStruct((depth, cvec.shape[0], n6), F32),
        compiler_params=_params(("arbitrary", "arbitrary")),
        name="modulation",
    )(cvec, ada_w, ada_b.reshape(depth, 1, n6))


def _rope(x, cos, sin):
    w = x.shape[1]
    lane = lax.broadcasted_iota(jnp.int32, (1, w), 1)
    first = (lane % 64) < 32
    partner = jnp.where(first, pltpu.roll(x, w - 32, 1), pltpu.roll(x, 32, 1))
    return x * cos + partner * sin


def _head_rms64(x, bd, g):
    ms = _mm_split(x * x, bd)
    return x * lax.rsqrt(ms + NORM_EPS) * g


def _in_kernel(latent, *refs):
    if latent:
        (x_ref, g1_ref, sc_ref, sh_ref, wa_ref, ba_ref, wg_ref, bg_ref, fb_ref, bd_ref, qg_ref,
         kg_ref, cos_ref, sin_ref, mq_ref, mk_ref, mv_ref, nq_ref, nk_ref, nv_ref, gate_ref) = refs
    else:
        (x_ref, g1_ref, sc_ref, sh_ref, wa_ref, ba_ref, wg_ref, bg_ref, fb_ref, bd_ref, qg_ref,
         kg_ref, mq_ref, mk_ref, mv_ref, nq_ref, nk_ref, nv_ref, gate_ref) = refs
    h = _rms_mod(x_ref[0], g1_ref[...], sc_ref[0], sh_ref[0]).astype(_MXU)

    def seg(j):
        sl = slice(j * 512, (j + 1) * 512)
        return jnp.dot(h, wa_ref[:, sl], preferred_element_type=F32) + ba_ref[:, sl]

    mq, mk = seg(0), seg(1)
    if latent:
        cos = jnp.concatenate([cos_ref[...]] * ML_HEADS, axis=1)
        sin = jnp.concatenate([sin_ref[...]] * ML_HEADS, axis=1)
        mq = _rope(mq, cos, sin)
        mk = _rope(mk, cos, sin)
    mq_ref[0] = mq.astype(mq_ref.dtype)
    mk_ref[0] = (mk * ML_HEAD_DIM ** -0.5).astype(mk_ref.dtype)
    mv_ref[0] = seg(2).astype(mv_ref.dtype)
    bd = bd_ref[...]
    nq_ref[0] = (_head_rms64(seg(3), bd, qg_ref[...]) * NA_HEAD_DIM ** -0.5).astype(nq_ref.dtype)
    nk_ref[0] = _head_rms64(seg(4), bd, kg_ref[...])
    nv_ref[0] = seg(5)
    pg = jnp.dot(h, wg_ref[...], preferred_element_type=F32) + bg_ref[...]
    lane = lax.broadcasted_iota(jnp.int32, (1, 2 * LANES), 1) % LANES
    is_f = (lane >= ML_HEADS) & (lane < 2 * ML_HEADS)
    pg = jnp.where(is_f, _log_sigmoid(pg + fb_ref[...]), pg)
    gate_ref[0, 0] = pg[:, :LANES]
    gate_ref[1, 0] = pg[:, LANES:]


def _in_proj(x, mod, lw, rope_tab):
    bsz, n, d = x.shape
    latent = rope_tab is not None
    tm = min(TOKEN_TILE, n)
    sc, sh = mod
    per_batch = sc.shape[0] > 1
    midx = (lambda b, i: (b, 0, 0)) if per_batch else (lambda b, i: (0, 0, 0))
    tok = pl.BlockSpec((1, tm, 512), lambda b, i: (b, i, 0))
    in_specs = [pl.BlockSpec((1, tm, d), lambda b, i: (b, i, 0)),
                _const_spec((1, d)),
                pl.BlockSpec((1, 1, d), midx), pl.BlockSpec((1, 1, d), midx),
                _const_spec(lw["wa"].shape), _const_spec(lw["ba"].shape),
                _const_spec(lw["wg"].shape), _const_spec(lw["bg"].shape), _const_spec(lw["fb"].shape),
                _const_spec(lw["bd"].shape), _const_spec((1, 512)), _const_spec((1, 512))]
    args = [x, lw["norm1_g"], sc, sh, lw["wa"], lw["ba"], lw["wg"], lw["bg"], lw["fb"], lw["bd"],
            lw["na_q_g"], lw["na_k_g"]]
    if latent:
        in_specs += [pl.BlockSpec((tm, LANES), lambda b, i: (i, 0))] * 2
        args += list(rope_tab)
    act = jax.ShapeDtypeStruct((bsz, n, 512), _MXU)
    f32 = jax.ShapeDtypeStruct((bsz, n, 512), F32)
    return pl.pallas_call(
        functools.partial(_in_kernel, latent),
        grid=(bsz, n // tm),
        in_specs=in_specs,
        out_specs=[tok] * 6 + [pl.BlockSpec((2, 1, tm, LANES), lambda b, i: (0, b, i, 0))],
        out_shape=[act, act, act, act, f32, f32, jax.ShapeDtypeStruct((2, bsz, n, LANES), F32)],
        compiler_params=_params(("parallel", "parallel")),
        name="in_proj_latent" if latent else "in_proj_context",
    )(*args)


def _mlstm_kernel(want_state, q_ref, k_ref, v_ref, g_ref, c0_ref, n0_ref, m0_ref, *rest):
    if want_state:
        h_ref, cn_ref, nn_ref, mn_ref, c_s, n_s, m_s = rest
    else:
        h_ref, c_s, n_s, m_s = rest
    d = pl.program_id(1)
    c = pl.program_id(2)

    @pl.when(c == 0)
    def _():
        c_s[...] = c0_ref[0, 0]
        n_s[...] = n0_ref[0, 0]
        m_s[...] = m0_ref[0, 0]

    ln = q_ref.shape[1]
    row = lax.broadcasted_iota(jnp.int32, (ln, ln), 0)
    col = lax.broadcasted_iota(jnp.int32, (ln, ln), 1)
    sgn = 1 - 2 * d
    mask = (row - col) * sgn >= 0
    maskf = mask.astype(_MXU)
    mask_t = (col - row) * sgn >= 0
    g = g_ref[0, 0]
    g_t = g.T
    bcol_all = _mm_exact_r(maskf, g)
    brow_all = _mm_exact_l(g_t[0:SUBLANES], mask_t.astype(_MXU))
    btot_all = jnp.sum(g, axis=0, keepdims=True)
    q = q_ref[0]
    k = k_ref[0]
    v = v_ref[0]
    for h in range(ML_HEADS):
        sl = slice(h * ML_HEAD_DIM, (h + 1) * ML_HEAD_DIM)
        qh, kh, vh = q[:, sl], k[:, sl], v[:, sl]
        bcol = bcol_all[:, ML_HEADS + h:ML_HEADS + h + 1]
        igcol = g[:, h:h + 1]
        brow = brow_all[ML_HEADS + h:ML_HEADS + h + 1, :]
        igrow = g_t[h:h + 1, :]
        btot = btot_all[:, ML_HEADS + h:ML_HEADS + h + 1]
        m_prev = m_s[h][:, 0:1]
        c_prev = c_s[h]
        n_prev = n_s[h]
        dlog = jnp.where(mask, bcol - brow + igrow, -jnp.inf)
        a = bcol + m_prev
        m_t = jnp.maximum(a, jnp.max(dlog, axis=-1, keepdims=True))
        inter = jnp.exp(a - m_t)
        s = _mm_nt(qh, kh) * jnp.exp(dlog - m_t)
        num = inter * _mm(qh, c_prev) + _mm(s, vh)
        den = (inter * jnp.sum(qh.astype(F32) * n_prev, axis=-1, keepdims=True)
               + jnp.sum(s, axis=-1, keepdims=True))
        h_ref[0, 0, :, sl] = (num / jnp.maximum(jnp.abs(den), jnp.exp(-m_t))).astype(h_ref.dtype)
        m_new = jnp.maximum(btot + m_prev,
                            jnp.max(btot - brow + igrow, axis=-1, keepdims=True))
        w_end = jnp.exp(btot - bcol + igcol - m_new)
        decay = jnp.exp(btot + m_prev - m_new)
        kw = kh.astype(F32) * w_end
        c_s[h] = decay * c_prev + _mm(kw.T, vh)
        n_s[h] = decay * n_prev + jnp.sum(kw, axis=0, keepdims=True)
        m_s[h] = jnp.broadcast_to(m_new, (1, LANES))

    if want_state:
        @pl.when(c == pl.num_programs(2) - 1)
        def _():
            cn_ref[0, 0] = c_s[...]
            nn_ref[0, 0] = n_s[...]
            mn_ref[0, 0] = m_s[...]


def _mlstm(q, k, v, gates, c0, n0, m0, want_state):
    bsz, n, _ = q.shape
    ln = min(ML_CHUNK, n)
    nc = n // ln
    hd = ML_HEAD_DIM
    n0 = n0.reshape(bsz, 2, ML_HEADS, 1, hd)
    m0 = jnp.broadcast_to(m0[..., None, None], (bsz, 2, ML_HEADS, 1, LANES))

    def chunk(b, d, c):
        return c + d * (nc - 1 - 2 * c)

    tok = pl.BlockSpec((1, ln, 512), lambda b, d, c: (b, chunk(b, d, c), 0))
    st_c = pl.BlockSpec((1, 1, ML_HEADS, hd, hd), lambda b, d, c: (b, d, 0, 0, 0))
    st_n = pl.BlockSpec((1, 1, ML_HEADS, 1, hd), lambda b, d, c: (b, d, 0, 0, 0))
    st_m = pl.BlockSpec((1, 1, ML_HEADS, 1, LANES), lambda b, d, c: (b, d, 0, 0, 0))
    out_specs = [pl.BlockSpec((1, 1, ln, 512), lambda b, d, c: (d, b, chunk(b, d, c), 0))]
    out_shape = [jax.ShapeDtypeStruct((2, bsz, n, 512), _MXU)]
    if want_state:
        out_specs += [st_c, st_n, st_m]
        out_shape += [jax.ShapeDtypeStruct((bsz, 2, ML_HEADS, hd, hd), F32),
                      jax.ShapeDtypeStruct((bsz, 2, ML_HEADS, 1, hd), F32),
                      jax.ShapeDtypeStruct((bsz, 2, ML_HEADS, 1, LANES), F32)]
    return pl.pallas_call(
        functools.partial(_mlstm_kernel, want_state),
        grid=(bsz, 2, nc),
        in_specs=[tok, tok, tok,
                  pl.BlockSpec((1, 1, ln, LANES), lambda b, d, c: (d, b, chunk(b, d, c), 0)),
                  st_c, st_n, st_m],
        out_specs=out_specs,
        out_shape=out_shape,
        scratch_shapes=[pltpu.VMEM((ML_HEADS, hd, hd), F32),
                        pltpu.VMEM((ML_HEADS, 1, hd), F32),
                        pltpu.VMEM((ML_HEADS, 1, LANES), F32)],
        compiler_params=_params(("parallel", "parallel", "arbitrary")),
        name="mlstm_state" if want_state else "mlstm",
    )(q, k, v, gates, c0, n0, m0)


def _head_mask(hh):
    lane = lax.broadcasted_iota(jnp.int32, (1, LANES), 1)
    return (lane >= NA_HEAD_DIM * hh) & (lane < NA_HEAD_DIM * (hh + 1))


def _na_ctx_kernel(q_ref, k_ref, v_ref, y_ref):
    q = q_ref[0]
    k = k_ref[0]
    v = v_ref[0]
    for j in range(NA_WIDTH // LANES):
        sl = slice(j * LANES, (j + 1) * LANES)
        q2 = q[:, sl]
        k2 = k[:, sl].astype(_MXU)
        v2 = v[:, sl].astype(_MXU)
        y2 = None
        for hh in range(2):
            hm = _head_mask(hh)
            s = _mm_nt(jnp.where(hm, q2, jnp.zeros_like(q2)), k2)
            p = jnp.exp(s - jnp.max(s, axis=-1, keepdims=True))
            o = _mm(p, v2) / jnp.sum(p, axis=-1, keepdims=True)
            y2 = o if hh == 0 else jnp.where(hm, o, y2)
        y_ref[0, :, sl] = y2


def _na_context(nq, nk, nv):
    bsz, n, w = nq.shape
    spec = pl.BlockSpec((1, n, w), lambda b: (b, 0, 0))
    return pl.pallas_call(
        _na_ctx_kernel,
        grid=(bsz,),
        in_specs=[spec, spec, spec],
        out_specs=spec,
        out_shape=jax.ShapeDtypeStruct((bsz, n, w), F32),
        compiler_params=_params(("parallel",)),
        name="na_context",
    )(nq, nk, nv)


def _na_lat_kernel(q_ref, k0, k1, k2, k3, v0, v1, v2, v3, ck_ref, cv_ref, t_ref, mask_ref, y_ref):
    rb = pl.program_id(1)
    n = pl.program_id(2)
    rows = pl.num_programs(1) * NA_QROWS
    s0 = jnp.clip(NA_QROWS * rb - NA_WIN_ROWS // 2, 0, rows - NA_KROWS)
    off_r = NA_QROWS * rb - s0
    bs = pl.multiple_of(jnp.clip(NA_QCOLS * n - NA_WIN_COLS // 2, 0, GRID_W - NA_KBAND), SUBLANES)
    off_c = pl.multiple_of(NA_QCOLS * n - bs, SUBLANES)
    nq_tile = NA_QROWS * NA_QCOLS
    q = q_ref[0].reshape(nq_tile, NA_WIDTH)
    madd = mask_ref[0, 0]
    for j in range(NA_WIDTH // LANES):
        sl = slice(j * LANES, (j + 1) * LANES)
        q2 = q[:, sl]
        kwin = jnp.concatenate(
            [r[0, 0, :, pl.ds(bs, NA_KBAND), sl].reshape(4 * NA_KBAND, LANES) for r in (k0, k1, k2, k3)],
            axis=0).astype(_MXU)
        vwin = jnp.concatenate(
            [r[0, 0, :, pl.ds(bs, NA_KBAND), sl].reshape(4 * NA_KBAND, LANES) for r in (v0, v1, v2, v3)],
            axis=0).astype(_MXU)
        ck2 = ck_ref[0, :, sl].astype(_MXU)
        cv2 = cv_ref[0, :, sl].astype(_MXU)
        y2 = None
        for hh in range(2):
            hm = _head_mask(hh)
            qm = jnp.where(hm, q2, jnp.zeros_like(q2))
            bias = t_ref[2 * j + hh, pl.ds(off_r, NA_QROWS), pl.ds(off_c, NA_QCOLS), :]
            sw = _mm_nt(qm, kwin) + bias.reshape(nq_tile, NA_KROWS * NA_KBAND) + madd
            sc = _mm_nt(qm, ck2)
            m = jnp.maximum(jnp.max(sw, axis=-1, keepdims=True), jnp.max(sc, axis=-1, keepdims=True))
            pw = jnp.exp(sw - m)
            pc = jnp.exp(sc - m)
            den = jnp.sum(pw, axis=-1, keepdims=True) + jnp.sum(pc, axis=-1, keepdims=True)
            o = (_mm(pw, vwin) + _mm(pc, cv2)) / den
            y2 = o if hh == 0 else jnp.where(hm, o, y2)
        y_ref[0, :, :, sl] = y2.reshape(NA_QROWS, NA_QCOLS, LANES)


def _na_tables(rpb, rows):
    a = np.arange(NA_KROWS)
    cc = np.arange(NA_KBAND)
    dr = np.clip(a[None, :] - a[:, None] + NA_WIN_ROWS - 1, 0, 2 * NA_WIN_ROWS - 2)
    dc = np.clip(cc[None, :] - cc[:, None] + NA_WIN_COLS - 1, 0, 2 * NA_WIN_COLS - 2)
    oh_r = (dr[:, :, None] == np.arange(2 * NA_WIN_ROWS - 1)).astype(np.float32)
    oh_c = (dc[:, :, None] == np.arange(2 * NA_WIN_COLS - 1)).astype(np.float32)
    table = jnp.einsum("aki,hij->hakj", oh_r, rpb, precision=_HI)
    table = jnp.einsum("hakj,cmj->hackm", table, oh_c, precision=_HI)
    table = table.reshape(NA_HEADS, NA_KROWS, NA_KBAND, NA_KROWS * NA_KBAND).astype(F32)
    nrb = rows // NA_QROWS
    ncb = GRID_W // NA_QCOLS
    kh = min(NA_WIN_ROWS, rows)
    r0 = np.arange(nrb) * NA_QROWS
    s0 = np.clip(r0 - NA_WIN_ROWS // 2, 0, rows - NA_KROWS)
    qr = r0[:, None] + np.arange(NA_QROWS)[None, :]
    kr = s0[:, None] + np.arange(NA_KROWS)[None, :]
    lo = np.clip(qr - NA_WIN_ROWS // 2, 0, rows - kh)
    row_ok = (kr[:, None, :] >= lo[:, :, None]) & (kr[:, None, :] < lo[:, :, None] + kh)
    c0 = np.arange(ncb) * NA_QCOLS
    bs = np.clip(c0 - NA_WIN_COLS // 2, 0, GRID_W - NA_KBAND)
    qc = c0[:, None] + np.arange(NA_QCOLS)[None, :]
    kc = bs[:, None] + np.arange(NA_KBAND)[None, :]
    qs = np.clip(qc - NA_WIN_COLS // 2, 0, GRID_W - NA_WIN_COLS)
    col_ok = (kc[:, None, :] >= qs[:, :, None]) & (kc[:, None, :] < qs[:, :, None] + NA_WIN_COLS)
    ok = row_ok[:, None, :, None, :, None] & col_ok[None, :, None, :, None, :]
    mask = np.where(ok, 0.0, NEG).astype(np.float32)
    mask = mask.reshape(nrb, ncb, NA_QROWS * NA_QCOLS, NA_KROWS * NA_KBAND)
    return table, jnp.asarray(mask)


def _na_latent(nq, nk, nv, ck, cv, table, mask):
    bsz, n, w = nq.shape
    rows = n // GRID_W
    assert rows % NA_QROWS == 0 and rows >= NA_KROWS
    nrb = rows // NA_QROWS
    ncb = GRID_W // NA_QCOLS
    ngr = rows // 4
    past = ck.shape[1]
    q4 = nq.reshape(bsz, rows, GRID_W, w)
    k5 = nk.reshape(bsz, ngr, 4, GRID_W, w)
    v5 = nv.reshape(bsz, ngr, 4, GRID_W, w)

    def kv_spec(i):
        def idx(b, rb, c):
            g0 = jnp.clip(2 * rb - 1, 0, ngr - 4)
            return (b, g0 + i, 0, 0, 0)
        return pl.BlockSpec((1, 1, 4, GRID_W, w), idx)

    qspec = pl.BlockSpec((1, NA_QROWS, NA_QCOLS, w), lambda b, rb, c: (b, rb, c, 0))
    cspec = pl.BlockSpec((1, past, w), lambda b, rb, c: (b, 0, 0))
    y = pl.pallas_call(
        _na_lat_kernel,
        grid=(bsz, nrb, ncb),
        in_specs=[qspec] + [kv_spec(i) for i in range(4)] + [kv_spec(i) for i in range(4)]
        + [cspec, cspec, _const_spec(table.shape),
           pl.BlockSpec((1, 1) + mask.shape[2:], lambda b, rb, c: (rb, c, 0, 0))],
        out_specs=qspec,
        out_shape=jax.ShapeDtypeStruct((bsz, rows, GRID_W, w), F32),
        compiler_params=_params(("parallel", "parallel", "arbitrary")),
        name="na_latent",
    )(q4, k5, k5, k5, k5, v5, v5, v5, v5, ck, cv, table, mask)
    return y.reshape(bsz, n, w)


def _merge_kernel(x_ref, xp_ref, xn_ref, hf_ref, hb_ref, yna_ref, g1_ref, sc1_ref, sh1_ref, gt1_ref,
                  w2_ref, b2_ref, mlg_ref, cvw_ref, wpa_ref, wpb_ref, wpc_ref, wo_ref,
                  g2_ref, sc2_ref, sh2_ref, wr_ref, x1_ref, h2_ref, aff_ref, afft_ref):
    i = pl.program_id(1)
    last = pl.num_programs(1) - 1
    tm = x_ref.shape[1]
    d = x_ref.shape[2]
    x = x_ref[0]
    g1, sc1, sh1 = g1_ref[...], sc1_ref[0], sh1_ref[0]
    h = _rms_mod(x, g1, sc1, sh1)
    hp = _rms_mod(xp_ref[0, 0], g1, sc1, sh1)
    hn = _rms_mod(xn_ref[0, 0], g1, sc1, sh1)
    hb16 = h.astype(_MXU)
    hext = jnp.concatenate([hp, h, hn], axis=0).astype(_MXU)

    def proj(lhs, lo, hi):
        return jnp.dot(lhs, w2_ref[:, lo:hi], preferred_element_type=F32) + b2_ref[:, lo:hi]

    mo = proj(hb16, 0, 512)
    cb = proj(hb16, 512, 1024)
    cch = proj(hext, 1024, 2048)
    u = cch[:, :512] * cch[:, 512:]
    rowi = lax.broadcasted_iota(jnp.int32, (tm + 2 * SUBLANES, 1), 0)
    pad = ((rowi == SUBLANES - 1) & (i == 0)) | ((rowi == tm + SUBLANES) & (i == last))
    u = jnp.where(pad, 0.0, u)
    cvw = cvw_ref[...]
    conv = (u[SUBLANES - 1:SUBLANES - 1 + tm] * cvw[0:1]
            + u[SUBLANES:SUBLANES + tm] * cvw[1:2]
            + u[SUBLANES + 1:SUBLANES + 1 + tm] * cvw[2:3])
    y_cv = cb * conv
    hml = hf_ref[0, 0].astype(F32) + hb_ref[0, 0].astype(F32)
    parts = []
    for hh in range(ML_HEADS):
        seg = hml[:, hh * ML_HEAD_DIM:(hh + 1) * ML_HEAD_DIM]
        parts.append(seg * lax.rsqrt(jnp.mean(seg * seg, axis=-1, keepdims=True) + NORM_EPS))
    y_ml = jnp.concatenate(parts, axis=1) * mlg_ref[...] * _sigmoid(mo)
    merged = _sigmoid(proj(hb16, 2048, 2048 + d)) * _mm(y_ml, wpa_ref[...])
    merged = merged + _sigmoid(proj(hb16, 2048 + d, 2048 + 2 * d)) * _mm(y_cv, wpb_ref[...])
    merged = merged + _sigmoid(proj(hb16, 2048 + 2 * d, 2048 + 3 * d)) * _mm(yna_ref[0], wpc_ref[...])
    x1 = x + gt1_ref[0] * _mm(merged, wo_ref[...])
    x1_ref[0] = x1
    h2 = _rms_mod(x1, g2_ref[...], sc2_ref[0], sh2_ref[0])
    h2_ref[0] = h2
    logits = _mm(h2, wr_ref[...])
    lane = lax.broadcasted_iota(jnp.int32, (1, LANES), 1)
    logits = jnp.where(lane < N_EXPERTS, logits, NEG)
    e = jnp.exp(logits - jnp.max(logits, axis=-1, keepdims=True))
    aff = e / jnp.sum(e, axis=-1, keepdims=True)
    aff_ref[0] = aff
    afft_ref[0] = aff.T[:N_EXPERTS]


def _merge(x, hdir, yna, mod, lw):
    bsz, n, d = x.shape
    tm = min(TOKEN_TILE, n)
    sc1, sh1, gt1, sc2, sh2 = mod
    per_batch = sc1.shape[0] > 1
    midx = (lambda b, i: (b, 0, 0)) if per_batch else (lambda b, i: (0, 0, 0))
    mspec = pl.BlockSpec((1, 1, d), midx)
    x8 = x.reshape(bsz, n // SUBLANES, SUBLANES, d)
    nb8 = tm // SUBLANES
    tok_d = pl.BlockSpec((1, tm, d), lambda b, i: (b, i, 0))
    tok_w = pl.BlockSpec((1, tm, 512), lambda b, i: (b, i, 0))
    names = ["w2", "b2", "ml_out_g", "cv_w", "w_pa", "w_pb", "w_pc", "w_o"]
    in_specs = ([tok_d,
                 pl.BlockSpec((1, 1, SUBLANES, d), lambda b, i: (b, jnp.maximum(i * nb8 - 1, 0), 0, 0)),
                 pl.BlockSpec((1, 1, SUBLANES, d),
                              lambda b, i: (b, jnp.minimum((i + 1) * nb8, n // SUBLANES - 1), 0, 0)),
                 pl.BlockSpec((1, 1, tm, 512), lambda b, i: (0, b, i, 0)),
                 pl.BlockSpec((1, 1, tm, 512), lambda b, i: (1, b, i, 0)),
                 tok_w, _const_spec((1, d)), mspec, mspec, mspec]
                + [_const_spec(lw[k].shape) for k in names]
                + [_const_spec((1, d)), mspec, mspec, _const_spec(lw["w_r"].shape)])
    args = ([x, x8, x8, hdir, hdir, yna, lw["norm1_g"], sc1, sh1, gt1] + [lw[k] for k in names]
            + [lw["norm2_g"], sc2, sh2, lw["w_r"]])
    return pl.pallas_call(
        _merge_kernel,
        grid=(bsz, n // tm),
        in_specs=in_specs,
        out_specs=[tok_d, tok_d, pl.BlockSpec((1, tm, LANES), lambda b, i: (b, i, 0)),
                   pl.BlockSpec((1, N_EXPERTS, tm), lambda b, i: (b, 0, i))],
        out_shape=[jax.ShapeDtypeStruct((bsz, n, d), F32), jax.ShapeDtypeStruct((bsz, n, d), F32),
                   jax.ShapeDtypeStruct((bsz, n, LANES), F32),
                   jax.ShapeDtypeStruct((bsz, N_EXPERTS, n), F32)],
        compiler_params=_params(("parallel", "parallel")),
        name="merge",
    )(*args)


def _route_kernel(cap, aff_ref, idx_ref, loc_s):
    a = aff_ref[0]
    ne, n = a.shape

    def count(m):
        return jnp.sum(jnp.where(m, 1.0, 0.0), axis=1, keepdims=True)

    def cond(s):
        return (s[2] > 0) & (s[3] < 512)

    def body(s):
        lo, hi, _, it = s
        in_c = (a >= lo) & (a < hi)
        cmin = jnp.min(jnp.where(in_c, a, BIG), axis=1, keepdims=True)
        cmax = jnp.max(jnp.where(in_c, a, -BIG), axis=1, keepdims=True)
        done = cmin >= cmax
        mid = cmin + (cmax - cmin) * 0.5
        mid = jnp.where(mid > cmin, mid, cmax)
        ok = count(a >= mid) >= cap
        lo = jnp.where(done, lo, jnp.where(ok, mid, lo))
        hi = jnp.where(done, hi, jnp.where(ok, hi, mid))
        return lo, hi, jnp.sum(jnp.where(done, 0, 1)), it + 1

    lo, hi, _, _ = lax.while_loop(
        cond, body, (jnp.zeros((ne, 1), F32), jnp.full((ne, 1), BIG, F32), jnp.int32(1), jnp.int32(0)))
    gt = a >= hi
    eq = (a >= lo) & (a < hi)
    need = cap - count(gt)
    tok = lax.broadcasted_iota(jnp.int32, (ne, n), 1)

    def isearch(_, lh):
        jl, jh = lh
        mid = jl + ((jh - jl) >> 1)
        ok = count(eq & (tok < mid)) >= need
        return jnp.where(ok, jl, mid), jnp.where(ok, mid, jh)

    _, jmax = lax.fori_loop(0, int(np.ceil(np.log2(n))) + 1, isearch,
                            (jnp.zeros((ne, 1), jnp.int32), jnp.full((ne, 1), n, jnp.int32)))
    self = jnp.where(gt | (eq & (tok < jmax)), 1.0, 0.0).astype(_MXU)
    nb = n // LANES
    tri = (lax.broadcasted_iota(jnp.int32, (LANES, LANES), 0)
           <= lax.broadcasted_iota(jnp.int32, (LANES, LANES), 1)).astype(_MXU)
    for k in range(nb):
        loc_s[k * ne:(k + 1) * ne, :] = jnp.dot(self[:, k * LANES:(k + 1) * LANES], tri,
                                                preferred_element_type=F32)
    blk = (lax.broadcasted_iota(jnp.int32, (n, LANES), 0) // LANES
           == lax.broadcasted_iota(jnp.int32, (n, LANES), 1)).astype(_MXU)
    bcnt = jnp.dot(self, blk, preferred_element_type=F32)
    bend = jnp.dot(bcnt.astype(_MXU), tri, preferred_element_type=F32)
    boff = bend - bcnt
    slot = lax.broadcasted_iota(jnp.int32, (cap, 1), 0).astype(F32)
    lane = lax.broadcasted_iota(jnp.int32, (1, LANES), 1)
    lanef = lane.astype(F32)
    out = jnp.zeros((cap, LANES), F32)
    for e in range(ne):
        kj = jnp.sum(jnp.where(bend[e:e + 1, :] <= slot, 1.0, 0.0), axis=1, keepdims=True)
        onek = lanef == kj
        jl = slot - jnp.sum(jnp.where(onek, boff[e:e + 1, :], 0.0), axis=1, keepdims=True)
        loc_e = loc_s[pl.ds(e, nb, stride=ne), :]
        if nb >= 16:
            rows = jnp.dot(jnp.where(onek, 1.0, 0.0)[:, :nb].astype(_MXU), loc_e.astype(_MXU),
                           preferred_element_type=F32)
        else:
            rows = jnp.zeros((cap, LANES), F32)
            for k in range(nb):
                rows = jnp.where(kj == k, loc_e[k:k + 1, :], rows)
        tl = jnp.sum(jnp.where(rows <= jl, 1.0, 0.0), axis=1, keepdims=True)
        out = jnp.where(lane == e, kj * LANES + tl, out)
    idx_ref[0] = out.astype(jnp.int32)


def _route(aff_t, cap):
    bsz, ne, n = aff_t.shape
    idx_t = pl.pallas_call(
        functools.partial(_route_kernel, cap),
        grid=(bsz,),
        in_specs=[pl.BlockSpec((1, ne, n), lambda b: (b, 0, 0))],
        out_specs=pl.BlockSpec((1, cap, LANES), lambda b: (b, 0, 0)),
        out_shape=jax.ShapeDtypeStruct((bsz, cap, LANES), jnp.int32),
        scratch_shapes=[pltpu.VMEM((n // LANES * ne, LANES), F32)],
        compiler_params=_params(("parallel",)),
        name="route",
    )(aff_t)
    return idx_t[:, :, :ne].transpose(0, 2, 1)


def _ffn_kernel(idx_ref, h2_ref, aff_ref, gt_ref, wg_ref, wu_ref, wd_ref, x1_ref, out_ref,
                xg_s, ag_s, y_s, acc_s, sem):
    o = pl.program_id(0)
    e = pl.program_id(1)
    last = pl.num_programs(1) - 1
    m, d = xg_s.shape
    ff = wg_ref.shape[2]

    def gather_row(base, r):
        t = idx_ref[0, 0, base + r]
        xg_s[pl.ds(r, 1), :] = h2_ref[0, pl.ds(t, 1), :]
        ag_s[pl.ds(r, 1), :] = aff_ref[0, pl.ds(t, 1), :]

    def scatter_rows(base, r0, nrow):
        ts = [idx_ref[0, 0, base + r0 + k] for k in range(nrow)]
        vals = [acc_s[pl.ds(ts[k], 1), :] + y_s[pl.ds(r0 + k, 1), :] for k in range(nrow)]
        for k in range(nrow):
            acc_s[pl.ds(ts[k], 1), :] = vals[k]

    @pl.when(e == 0)
    def _():
        cp = pltpu.make_async_copy(x1_ref.at[o], acc_s, sem)
        cp.start()
        y_s[...] = jnp.zeros_like(y_s)

        def body(r, carry):
            gather_row(0, r)
            return carry
        lax.fori_loop(0, m, body, 0, unroll=8)
        cp.wait()

    lane = lax.broadcasted_iota(jnp.int32, (1, LANES), 1)
    gate = jnp.sum(jnp.where(lane == e, ag_s[...], 0.0), axis=1, keepdims=True)
    gt2 = gt_ref[0]
    xg = xg_s[...].astype(_MXU)
    prev = jnp.maximum(e - 1, 0) * m
    pieces = []
    npc = FFN_PIECES
    wcol = ff // npc
    rows_a = m // (2 * npc)
    r = 0
    for w_ref in (wg_ref, wu_ref):
        for j in range(npc):
            pieces.append(jnp.dot(xg, w_ref[0, :, j * wcol:(j + 1) * wcol], preferred_element_type=F32))
            for _ in range(rows_a // FFN_SCATTER_GROUP):
                scatter_rows(prev, r, FFN_SCATTER_GROUP)
                r += FFN_SCATTER_GROUP
    a = jnp.concatenate(pieces[:npc], axis=1)
    u = jnp.concatenate(pieces[npc:], axis=1)
    act = (a * _sigmoid(a) * u).astype(_MXU)
    nxt = jnp.minimum(e + 1, last) * m
    dcol = d // npc
    rows_b = m // npc
    r = 0
    for j in range(npc):
        sl = slice(j * dcol, (j + 1) * dcol)
        y_s[:, sl] = jnp.dot(act, wd_ref[0, :, sl], preferred_element_type=F32) * gate * gt2[:, sl]
        for _ in range(rows_b):
            gather_row(nxt, r)
            r += 1

    @pl.when(e == last)
    def _():
        def body(g, carry):
            scatter_rows(last * m, g * FFN_SCATTER_GROUP, FFN_SCATTER_GROUP)
            return carry
        lax.fori_loop(0, m // FFN_SCATTER_GROUP, body, 0, unroll=2)
        cp = pltpu.make_async_copy(acc_s, out_ref.at[o], sem)
        cp.start()
        cp.wait()


def _ffn(x1, h2, aff, gt2, idx, lw):
    outer, r, d = h2.shape
    m = idx.shape[2]
    ff = lw["w_gate"].shape[2]
    idx = idx.reshape(outer, 1, N_EXPERTS * m)
    gidx = (lambda o, e: (o, 0, 0)) if gt2.shape[0] > 1 else (lambda o, e: (0, 0, 0))
    return pl.pallas_call(
        _ffn_kernel,
        grid=(outer, N_EXPERTS),
        in_specs=[pl.BlockSpec((1, 1, N_EXPERTS * m), lambda o, e: (o, 0, 0), memory_space=pltpu.SMEM),
                  pl.BlockSpec((1, r, d), lambda o, e: (o, 0, 0), pipeline_mode=pl.Buffered(1)),
                  pl.BlockSpec((1, r, LANES), lambda o, e: (o, 0, 0), pipeline_mode=pl.Buffered(1)),
                  pl.BlockSpec((1, 1, d), gidx),
                  pl.BlockSpec((1, d, ff), lambda o, e: (e, 0, 0)),
                  pl.BlockSpec((1, d, ff), lambda o, e: (e, 0, 0)),
                  pl.BlockSpec((1, ff, d), lambda o, e: (e, 0, 0)),
                  pl.BlockSpec(memory_space=pl.ANY)],
        out_specs=pl.BlockSpec(memory_space=pl.ANY),
        out_shape=jax.ShapeDtypeStruct((outer, r, d), F32),
        scratch_shapes=[pltpu.VMEM((m, d), F32), pltpu.VMEM((m, LANES), F32), pltpu.VMEM((m, d), F32),
                        pltpu.VMEM((r, d), F32), pltpu.SemaphoreType.DMA(())],
        compiler_params=_params(("arbitrary", "arbitrary")),
        name="expert_ffn",
    )(idx, h2, aff, gt2, lw["w_gate"], lw["w_up"], lw["w_down"], x1)


def _expert_ffn(x1, h2, aff, aff_t, gt2, lw, group_all):
    bsz, n, d = x1.shape
    cap = EC_CAPACITY_FACTOR * n // N_EXPERTS
    idx = _route(aff_t, cap)
    inner = bsz if group_all else 1
    outer = bsz // inner
    rows = (jnp.arange(inner, dtype=jnp.int32) * n)[None, :, None, None]
    idx = idx.reshape(outer, inner, N_EXPERTS, cap) + rows
    idx = idx.transpose(0, 2, 1, 3).reshape(outer, N_EXPERTS, inner * cap)
    x2 = _ffn(x1.reshape(outer, inner * n, d), h2.reshape(outer, inner * n, d),
              aff.reshape(outer, inner * n, LANES), gt2, idx, lw)
    return x2.reshape(bsz, n, d)


def _cast_kernel(x_ref, o_ref):
    o_ref[...] = x_ref[...].astype(o_ref.dtype)


def _cast_experts(w):
    ne, k, n = w.shape
    spec = pl.BlockSpec((1, k, n), lambda e: (e, 0, 0))
    return pl.pallas_call(
        _cast_kernel, grid=(ne,), in_specs=[spec], out_specs=spec,
        out_shape=jax.ShapeDtypeStruct(w.shape, _MXU),
        compiler_params=_params(("parallel",)), name="cast_experts",
    )(w)


def _rope_tables(n):
    t = jnp.arange(n)
    row = (t // GRID_W).astype(F32)
    col = (t % GRID_W).astype(F32)
    half = ML_HEAD_DIM // 4
    freqs = ROPE_BASE ** (-jnp.arange(half, dtype=F32) / half)
    ar = row[:, None] * freqs[None, :]
    ac = col[:, None] * freqs[None, :]
    cos = jnp.concatenate([jnp.cos(ar), jnp.cos(ar), jnp.cos(ac), jnp.cos(ac)], axis=1)
    sin = jnp.concatenate([-jnp.sin(ar), jnp.sin(ar), -jnp.sin(ac), jnp.sin(ac)], axis=1)
    return cos, sin


def _layer_weights(l, p):
    d = p["w_in"].shape[1]
    offs = np.concatenate([[0], np.cumsum(PROJ_SIZES)]).tolist()
    w_in, b_in = p["w_in"][l], p["b_in"][l]
    col = lambda i: w_in[:, offs[i]:offs[i + 1]]
    bcol = lambda i: b_in[offs[i]:offs[i + 1]]
    gt_w, gt_b = w_in[:, offs[12]:], b_in[offs[12]:]
    a_ids = (0, 1, 2, 9, 10, 11)
    wa = jnp.concatenate([col(i) for i in a_ids], axis=1).astype(_MXU)
    ba = jnp.concatenate([bcol(i) for i in a_ids])[None, :]
    w2 = jnp.concatenate([col(3), col(6), col(7), col(8), gt_w], axis=1).astype(_MXU)
    b2 = jnp.concatenate([bcol(3), bcol(6), bcol(7), bcol(8), gt_b])[None, :]
    mi_w, mf_w, mi_b, mf_b = col(4), col(5), bcol(4), bcol(5)
    h = ML_HEADS
    zw = jnp.zeros((d, LANES - 2 * h), F32)
    zb = jnp.zeros((LANES - 2 * h,), F32)
    wg = jnp.concatenate([mi_w[:, :h], mf_w[:, :h], zw, mi_w[:, h:], mf_w[:, h:], zw], axis=1).astype(_MXU)
    bg = jnp.concatenate([mi_b[:h], mf_b[:h], zb, mi_b[h:], mf_b[h:], zb])[None, :]
    zh = jnp.zeros((h,), F32)
    fbias = p["ml_fbias"][l]
    fb = jnp.concatenate([zh, fbias[0], zb, zh, fbias[1], zb])[None, :]
    hid = np.arange(NA_WIDTH) // NA_HEAD_DIM
    bd = jnp.asarray((hid[:, None] == hid[None, :]).astype(np.float32) / NA_HEAD_DIM).astype(_MXU)
    w_r = jnp.concatenate([p["w_router"][l], jnp.zeros((d, LANES - N_EXPERTS), F32)], axis=1)
    return dict(
        wa=wa, ba=ba, w2=w2, b2=b2, wg=wg, bg=bg, fb=fb, bd=bd, w_r=w_r,
        norm1_g=p["norm1_g"][l][None, :], norm2_g=p["norm2_g"][l][None, :],
        na_q_g=jnp.tile(p["na_q_g"][l], NA_HEADS)[None, :], na_k_g=jnp.tile(p["na_k_g"][l], NA_HEADS)[None, :],
        ml_out_g=p["ml_out_g"][l][None, :], cv_w=p["cv_w"][l],
        w_pa=p["w_pa"][l].astype(_MXU), w_pb=p["w_pb"][l].astype(_MXU), w_pc=p["w_pc"][l].astype(_MXU),
        w_o=p["w_o"][l].astype(_MXU),
        w_gate=_cast_experts(p["w_gate"][l]), w_up=_cast_experts(p["w_up"][l]),
        w_down=_cast_experts(p["w_down"][l]))


def _sublayers(x, mod6, lw, ml_state, na_ctx, rope_tab, want_state, group_all):
    sh1, sc1, gt1, sh2, sc2, gt2 = mod6
    mq, mk, mv, nq, nk, nv, gates = _in_proj(x, (sc1, sh1), lw, rope_tab)
    ml = _mlstm(mq, mk, mv, gates, *ml_state, want_state)
    if na_ctx is None:
        yna = _na_context(nq, nk, nv)
    else:
        yna = _na_latent(nq, nk, nv, *na_ctx)
    x1, h2, aff, aff_t = _merge(x, ml[0], yna, (sc1, sh1, gt1, sc2, sh2), lw)
    x2 = _expert_ffn(x1, h2, aff, aff_t, gt2, lw, group_all)
    return x2, ml[1:], nk, nv


def kernel(x_prompt, x_sample, c, cache_ctx_k, cache_ctx_v, state_mlstm_c, state_mlstm_n, state_mlstm_m,
           c_ctx, ada_w, ada_b, norm1_g, norm2_g, w_in, b_in, ml_fbias, ml_out_g, cv_w, na_q_g, na_k_g,
           na_rpb, w_pa, w_pb, w_pc, w_o, w_router, w_gate, w_up, w_down):
    p = dict(w_in=w_in, b_in=b_in, ml_fbias=ml_fbias, ml_out_g=ml_out_g, cv_w=cv_w, na_q_g=na_q_g,
             na_k_g=na_k_g, w_pa=w_pa, w_pb=w_pb, w_pc=w_pc, w_o=w_o, w_router=w_router, w_gate=w_gate,
             w_up=w_up, w_down=w_down, norm1_g=norm1_g, norm2_g=norm2_g)
    depth = ada_w.shape[0]
    bp, np_tok, d = x_prompt.shape
    bs, ns_tok, _ = x_sample.shape
    hd = ML_HEAD_DIM
    nrow = -(-(bs + 1) // SUBLANES) * SUBLANES
    cvec = jnp.zeros((nrow, d), F32).at[:bs].set(c).at[bs].set(c_ctx)
    mod = _modulation(cvec, ada_w, ada_b)
    rope_tab = _rope_tables(ns_tok)
    zero_state = (jnp.zeros((bp, 2, ML_HEADS, hd, hd), F32), jnp.zeros((bp, 2, ML_HEADS, hd), F32),
                  jnp.zeros((bp, 2, ML_HEADS), F32))
    xp, xs = x_prompt, x_sample
    ks_l, vs_l, cs_l, ns_l, ms_l = [], [], [], [], []
    for l in range(depth):
        lw = _layer_weights(l, p)
        m6 = mod[l].reshape(nrow, 6, d)
        mod_p = [m6[bs:bs + 1, i][:, None, :] for i in range(6)]
        mod_s = [m6[:bs, i][:, None, :] for i in range(6)]
        xp, (cst, nst, mst), nk, nv = _sublayers(xp, mod_p, lw, zero_state, None, None, True, True)
        ks_l.append(nk.reshape(bp, np_tok, NA_HEADS, NA_HEAD_DIM))
        vs_l.append(nv.reshape(bp, np_tok, NA_HEADS, NA_HEAD_DIM))
        cs_l.append(cst)
        ns_l.append(nst[:, :, :, 0, :])
        ms_l.append(mst[:, :, :, 0, 0])
        table, mask = _na_tables(na_rpb[l], ns_tok // GRID_W)
        past = cache_ctx_k.shape[2]
        na_ctx = (cache_ctx_k[:, l].reshape(bs, past, NA_WIDTH).astype(_MXU),
                  cache_ctx_v[:, l].reshape(bs, past, NA_WIDTH).astype(_MXU), table, mask)
        ml_state = (state_mlstm_c[:, l], state_mlstm_n[:, l], state_mlstm_m[:, l])
        xs, _, _, _ = _sublayers(xs, mod_s, lw, ml_state, na_ctx, rope_tab, False, False)
    return (xp, xs, jnp.stack(ks_l, axis=1), jnp.stack(vs_l, axis=1), jnp.stack(cs_l, axis=1),
            jnp.stack(ns_l, axis=1), jnp.stack(ms_l, axis=1))
```
